```python
import math
import jax, jax.numpy as jnp
from jax import lax
import numpy as np

D_MODEL = 1024
BATCH = 4
SEQ = 4096
DEPTH = 4
DEC_BATCH = 128
DEC_SEQ = 8
PAST_LEN = 2048
PAGE_SIZE = 128

N_EVEN = (DEPTH + 1) // 2
N_ODD = DEPTH // 2
N_MEM = 256

H_A = 4
DK_A = 128
DV_A = 128
W_A = H_A * DV_A
CHUNK_A = 64
B_WINDOWS = (128, 512, 2048)
B_DILATIONS = (1, 4, 16)
N_BG = 3
H_B = 8
DH_B = 64
W_B = H_B * DH_B
ROT_DIM_B = DH_B // 4
ROPE_THETA = 500000.0
H_C = 16
P_C = 64
D_INNER_C = H_C * P_C
G_C = 2
N_C = 128
CONV_W = 4
CONV_DIM_C = D_INNER_C + 2 * G_C * N_C
CHUNK_C = 64
H_D = 4
DK_D = 64
DV_D = 128
W_D = H_D * DV_D
CHUNK_D = 64
RET_THETA = 10000.0
H_M = 4
DH_M = D_MODEL // H_M
ALPHA = (2.0 * DEPTH) ** 0.25
BETA = (8.0 * DEPTH) ** -0.25
LN_EPS = 1e-5
RMS_EPS = 1e-6
MASK_NEG = -1e30

EVEN_SIZES = (H_A * DK_A, H_A * DK_A, W_A, W_A, N_BG * W_B, N_BG * W_B, N_BG * W_B, W_B)
ODD_SIZES = (D_INNER_C, CONV_DIM_C, H_C, H_D * DK_D, H_D * DK_D, W_D, W_D)

kernel_name = 'hybrid_hgrn2_dilated_ssd_retention_step'


def split_cols(t, sizes):
    out, o = [], 0
    for s in sizes:
        out.append(t[..., o:o + s])
        o += s
    return out


def layer_norm(x, g, b):
    xf = x.astype(jnp.float32)
    mu = jnp.mean(xf, -1, keepdims=True)
    var = jnp.mean(jnp.square(xf - mu), -1, keepdims=True)
    return ((xf - mu) * lax.rsqrt(var + LN_EPS) * g + b).astype(x.dtype)


def rms_norm(x, g):
    xf = x.astype(jnp.float32)
    return (xf * lax.rsqrt(jnp.mean(xf * xf, -1, keepdims=True) + RMS_EPS) * g).astype(x.dtype)


def rotary(x, pos, rot_dim, theta):
    half = rot_dim // 2
    inv_freq = theta ** (-jnp.arange(half, dtype=jnp.float32) / half)
    ang = pos.astype(jnp.float32)[:, None] * inv_freq
    cos, sin = jnp.cos(ang)[:, None, :], jnp.sin(ang)[:, None, :]
    xr = x[..., :rot_dim].astype(jnp.float32)
    x1, x2 = xr[..., :half], xr[..., half:]
    rot = jnp.concatenate([x1 * cos - x2 * sin, x1 * sin + x2 * cos], -1).astype(x.dtype)
    return jnp.concatenate([rot, x[..., rot_dim:]], -1)


def _pad_time(t, pad):
    return jnp.pad(t, [(0, 0), (0, pad)] + [(0, 0)] * (t.ndim - 2))


def _chunks(t, n, c):
    return jnp.moveaxis(t.reshape((t.shape[0], n, c) + t.shape[2:]), 1, 0)


def gla_scan(q, k, v, log_f, s0, chunk):
    bsz, t_len, h, _ = q.shape
    dv = v.shape[-1]
    c = min(chunk, t_len)
    n = -(-t_len // c)
    pad = n * c - t_len
    qc, kc, vc, gc = [_chunks(_pad_time(a.astype(jnp.float32), pad), n, c) for a in (q, k, v, log_f)]
    causal = jnp.tril(jnp.ones((c, c), dtype=bool))[None, :, :, None, None]

    def step(s, inp):
        qi, ki, vi, gi = inp
        b = jnp.cumsum(gi, axis=1)
        o_inter = jnp.einsum('bthk,bhkv->bthv', qi * jnp.exp(b), s)
        decay = jnp.exp(jnp.where(causal, b[:, :, None] - b[:, None, :], MASK_NEG))
        att = jnp.einsum('bthk,bshk,btshk->bhts', qi, ki, decay)
        o_intra = jnp.einsum('bhts,bshv->bthv', att, vi)
        b_last = b[:, -1]
        s_new = jnp.exp(b_last)[..., None] * s + jnp.einsum('bshk,bshv->bhkv', ki * jnp.exp(b_last[:, None] - b), vi)
        return s_new, o_inter + o_intra

    s_fin, o = lax.scan(step, s0.astype(jnp.float32), (qc, kc, vc, gc))
    o = jnp.moveaxis(o, 0, 1).reshape(bsz, n * c, h, dv)[:, :t_len]
    return o, s_fin


def scalar_decay_scan(q, k, v, log_a, s0, chunk):
    bsz, t_len, h, _ = q.shape
    dv = v.shape[-1]
    c = min(chunk, t_len)
    n = -(-t_len // c)
    pad = n * c - t_len
    qc, kc, vc, ac = [_chunks(_pad_time(a.astype(jnp.float32), pad), n, c) for a in (q, k, v, log_a)]
    causal = jnp.tril(jnp.ones((c, c), dtype=bool))[None, None]

    def step(s, inp):
        qi, ki, vi, ai = inp
        b = jnp.cumsum(ai, axis=1)
        bh = jnp.swapaxes(b, 1, 2)
        o_inter = jnp.einsum('bthk,bhkv->bthv', qi, s) * jnp.exp(b)[..., None]
        decay = jnp.exp(jnp.where(causal, bh[:, :, :, None] - bh[:, :, None, :], MASK_NEG))
        att = jnp.einsum('bthk,bshk->bhts', qi, ki) * decay
        o_intra = jnp.einsum('bhts,bshv->bthv', att, vi)
        b_last = b[:, -1]
        w = jnp.exp(b_last[:, None] - b)
        s_new = jnp.exp(b_last)[..., None, None] * s + jnp.einsum('bshk,bshv->bhkv', ki * w[..., None], vi)
        return s_new, o_inter + o_intra

    s_fin, o = lax.scan(step, s0.astype(jnp.float32), (qc, kc, vc, ac))
    o = jnp.moveaxis(o, 0, 1).reshape(bsz, n * c, h, dv)[:, :t_len]
    return o, s_fin


def _softmax_lse(s):
    m = jnp.max(s, -1, keepdims=True)
    e = jnp.exp(s - m)
    den = jnp.sum(e, -1, keepdims=True)
    return e / den, (m + jnp.log(den))[..., 0]


def dilated_attn_band(q, k, v, window, dil):
    bsz, t_len, h, dh = q.shape
    span = window // dil
    blk = span
    m = -(-t_len // dil)
    nb = -(-m // blk)
    t_pad = nb * blk * dil

    def deinterleave(a):
        a = _pad_time(a, t_pad - t_len).reshape(bsz, nb * blk, dil, h, dh)
        return jnp.moveaxis(a, 2, 1).reshape(bsz, dil, nb, blk, h, dh)

    def with_prev(a):
        prev = jnp.pad(a, ((0, 0), (0, 0), (1, 0), (0, 0), (0, 0), (0, 0)))[:, :, :-1]
        return jnp.concatenate([prev, a], axis=3)

    qd = deinterleave(q)
    kk, vv = with_prev(deinterleave(k)), with_prev(deinterleave(v))
    s = jnp.einsum('brnqhd,brnkhd->brnhqk', qd, kk, preferred_element_type=jnp.float32) * dh ** -0.5
    i = jnp.arange(blk)[:, None]
    j = jnp.arange(2 * blk)[None, :]
    dist = i + blk - j
    band = (dist >= 0) & (dist <= span)
    valid = band[None] & ((jnp.arange(nb)[:, None, None] > 0) | (j >= blk)[None])
    s = jnp.where(valid[:, None], s, MASK_NEG)
    p, lse = _softmax_lse(s)
    o = jnp.einsum('brnhqk,brnkhd->brnqhd', p, vv.astype(jnp.float32))

    def interleave(a):
        a = jnp.moveaxis(a.reshape((bsz, dil, nb * blk) + a.shape[4:]), 1, 2)
        return a.reshape((bsz, t_pad) + a.shape[3:])[:, :t_len]

    return interleave(o), interleave(jnp.swapaxes(lse, -1, -2))


def dilated_attn_gather(q, k_all, v_all, window, dil):
    bsz, t_len, h, dh = q.shape
    past = k_all.shape[1] - t_len
    span = window // dil
    idx = past + jnp.arange(t_len)[:, None] - dil * jnp.arange(span + 1)[None, :]
    valid = idx >= 0
    idx = jnp.maximum(idx, 0)
    kg = jnp.take(k_all, idx, axis=1)
    vg = jnp.take(v_all, idx, axis=1)
    s = jnp.einsum('bthd,btkhd->bhtk', q, kg, preferred_element_type=jnp.float32) * dh ** -0.5
    s = jnp.where(valid[None, None], s, MASK_NEG)
    p, lse = _softmax_lse(s)
    o = jnp.einsum('bhtk,btkhd->bthd', p, vg.astype(jnp.float32))
    return o, jnp.swapaxes(lse, 1, 2)


def merge_by_denominator(outs, lses):
    w = jax.nn.softmax(jnp.stack(lses), axis=0)
    return jnp.einsum('gbth,gbthd->bthd', w, jnp.stack(outs))


def depthwise_causal_conv(xp, w, b):
    y = lax.conv_general_dilated(xp, w[:, None, :], window_strides=(1,), padding='VALID',
                                 dimension_numbers=('NWC', 'WIO', 'NWC'), feature_group_count=xp.shape[-1])
    return y + b


def even_mixer(x, pos, lb, w_in, a_norm_g, w_out, s_a0, b_bufs):
    bsz, t_len, _ = x.shape
    f32 = jnp.float32
    qa, fa, ia, ga, qb, kb, vb, gb = split_cols(jnp.einsum('btd,de->bte', x, w_in), EVEN_SIZES)
    hd = lambda a, d: a.reshape(bsz, t_len, -1, d)
    fx = fa.astype(f32)
    log_f = jnp.log(lb + (1.0 - lb) * jax.nn.sigmoid(fx))
    k_a = (1.0 - lb) * jax.nn.sigmoid(-fx)
    o_a, s_a = gla_scan(hd(jax.nn.silu(qa.astype(f32)), DK_A), hd(k_a, DK_A), hd(ia, DV_A), hd(log_f, DK_A), s_a0, CHUNK_A)
    o_a = rms_norm(o_a, a_norm_g).reshape(bsz, t_len, W_A) * jax.nn.silu(ga.astype(f32))
    qb = rotary(hd(qb, DH_B), pos, ROT_DIM_B, ROPE_THETA).reshape(bsz, t_len, N_BG, H_B, DH_B)
    kb = rotary(hd(kb, DH_B), pos, ROT_DIM_B, ROPE_THETA).reshape(bsz, t_len, N_BG, H_B, DH_B)
    vb = vb.reshape(bsz, t_len, N_BG, H_B, DH_B)
    outs, lses, rows = [], [], []
    for g in range(N_BG):
        win, dil = B_WINDOWS[g], B_DILATIONS[g]
        if b_bufs is None:
            o, l = dilated_attn_band(qb[:, :, g], kb[:, :, g], vb[:, :, g], win, dil)
            keep = min(win, t_len)
            rows.append(jnp.stack([kb[:, t_len - keep:, g], vb[:, t_len - keep:, g]], axis=2))
        else:
            buf = b_bufs[g].astype(kb.dtype)
            k_all = jnp.concatenate([buf[:, :, 0], kb[:, :, g]], axis=1)
            v_all = jnp.concatenate([buf[:, :, 1], vb[:, :, g]], axis=1)
            o, l = dilated_attn_gather(qb[:, :, g], k_all, v_all, win, dil)
            rows.append(jnp.stack([kb[:, :, g], vb[:, :, g]], axis=2))
        outs.append(o)
        lses.append(l)
    o_b = merge_by_denominator(outs, lses).reshape(bsz, t_len, W_B) * jax.nn.silu(gb.astype(f32))
    out = jnp.einsum('bte,ed->btd', jnp.concatenate([o_a, o_b], -1).astype(x.dtype), w_out)
    return out, s_a, rows


def odd_mixer(x, pos, w_in, conv_w, conv_b, dt_bias, a_log, d_skip, c_norm_g, w_out, s_c0, conv0, s_d0):
    bsz, t_len, _ = x.shape
    f32 = jnp.float32
    z, xbc, dt, qd, kd, vd, gd = split_cols(jnp.einsum('btd,de->bte', x, w_in), ODD_SIZES)
    xbc_ext = jnp.concatenate([conv0.astype(xbc.dtype), xbc], axis=1)
    conv_new = xbc_ext[:, xbc_ext.shape[1] - (CONV_W - 1):]
    xbc = jax.nn.silu(depthwise_causal_conv(xbc_ext, conv_w.astype(xbc.dtype), conv_b).astype(f32))
    xs, bm, cm = split_cols(xbc, (D_INNER_C, G_C * N_C, G_C * N_C))
    dt = jax.nn.softplus(dt.astype(f32) + dt_bias)
    log_a = dt * -jnp.exp(a_log.astype(f32))
    xh = xs.reshape(bsz, t_len, H_C, P_C)
    rep = lambda a: jnp.repeat(a.reshape(bsz, t_len, G_C, N_C), H_C // G_C, axis=2)
    y, s_c = scalar_decay_scan(rep(cm), rep(bm), xh * dt[..., None], log_a, s_c0, CHUNK_C)
    y = ((y + d_skip[:, None] * xh).reshape(bsz, t_len, D_INNER_C) * jax.nn.silu(z.astype(f32))).reshape(bsz, t_len, G_C, -1)
    y = (y * lax.rsqrt(jnp.mean(y * y, -1, keepdims=True) + RMS_EPS)).reshape(bsz, t_len, D_INNER_C) * c_norm_g
    qr = rotary(qd.reshape(bsz, t_len, H_D, DK_D), pos, DK_D, RET_THETA)
    kr = rotary(kd.reshape(bsz, t_len, H_D, DK_D), pos, DK_D, RET_THETA) * DK_D ** -0.5
    log_gamma = jnp.log1p(-jnp.exp2(-5.0 - jnp.arange(H_D, dtype=f32)))
    o_d, s_d = scalar_decay_scan(qr, kr, vd.reshape(bsz, t_len, H_D, DV_D),
                                 jnp.broadcast_to(log_gamma, (bsz, t_len, H_D)), s_d0, CHUNK_D)
    mu = jnp.mean(o_d, -1, keepdims=True)
    var = jnp.mean(jnp.square(o_d - mu), -1, keepdims=True)
    o_d = ((o_d - mu) * lax.rsqrt(var + LN_EPS)).reshape(bsz, t_len, W_D) * jax.nn.silu(gd.astype(f32))
    out = jnp.einsum('bte,ed->btd', jnp.concatenate([y, o_d], -1).astype(x.dtype), w_out)
    return out, s_c, conv_new, s_d


def memory_kv(mem, w_kv):
    bsz, m, _ = mem.shape
    return jnp.einsum('bmd,de->bme', mem, w_kv).reshape(bsz, m, 2, H_M, DH_M)


def memory_attend(x, kv, w_q, w_o):
    bsz, t_len, _ = x.shape
    q = jnp.einsum('btd,de->bte', x, w_q).reshape(bsz, t_len, H_M, DH_M)
    s = jnp.einsum('bthd,bmhd->bhtm', q, kv[:, :, 0].astype(q.dtype), preferred_element_type=jnp.float32) * DH_M ** -0.5
    p = jax.nn.softmax(s, axis=-1)
    o = jnp.einsum('bhtm,bmhd->bthd', p, kv[:, :, 1].astype(jnp.float32))
    return jnp.einsum('bte,ed->btd', o.reshape(bsz, t_len, H_M * DH_M).astype(x.dtype), w_o)


def setup_inputs(seed: int = 0) -> dict:
    f32 = jnp.float32
    keys = iter(jax.random.split(jax.random.key(seed), 40))

    def nrm(shape, scale):
        return jax.random.normal(next(keys), shape, f32) * scale

    def gain(shape):
        return 1.0 + nrm(shape, 0.02)

    l1, l2, l3 = [min(w, PAST_LEN) for w in B_WINDOWS]
    x_prompt = nrm((BATCH, SEQ, D_MODEL), 1.0)
    x_sample = nrm((DEC_BATCH, DEC_SEQ, D_MODEL), 1.0)
    state_a = nrm((N_EVEN, DEC_BATCH, H_A, DK_A, DV_A), 0.5)
    cache_b1 = nrm((N_EVEN, DEC_BATCH, l1, 2, H_B, DH_B), 1.0)
    cache_b2 = nrm((N_EVEN, DEC_BATCH, l2, 2, H_B, DH_B), 1.0)
    cache_b3 = nrm((N_EVEN, DEC_BATCH, l3, 2, H_B, DH_B), 1.0)
    state_c_ssm = nrm((N_ODD, DEC_BATCH, H_C, N_C, P_C), 0.1)
    state_c_conv = nrm((N_ODD, DEC_BATCH, CONV_W - 1, CONV_DIM_C), 1.0)
    state_d = nrm((N_ODD, DEC_BATCH, H_D, DK_D, DV_D), 1.0)
    cache_mem_kv = nrm((DEPTH, DEC_BATCH, N_MEM, 2, H_M, DH_M), 1.0)
    mem_prompt = nrm((BATCH, N_MEM, D_MODEL), 1.0)
    w_in_even = nrm((N_EVEN, D_MODEL, sum(EVEN_SIZES)), D_MODEL ** -0.5)
    a_lb_logits = nrm((N_EVEN, H_A * DK_A), 0.1)
    a_norm_g = gain((N_EVEN, DV_A))
    w_out_even = nrm((N_EVEN, W_A + W_B, D_MODEL), BETA * (W_A + W_B) ** -0.5)
    w_in_odd = nrm((N_ODD, D_MODEL, sum(ODD_SIZES)), D_MODEL ** -0.5)
    c_conv_w = nrm((N_ODD, CONV_W, CONV_DIM_C), CONV_W ** -0.5)
    c_conv_b = nrm((N_ODD, CONV_DIM_C), 0.01)
    dt0 = jnp.exp(jax.random.uniform(next(keys), (N_ODD, H_C), f32, math.log(1e-3), math.log(1e-1)))
    c_dt_bias = dt0 + jnp.log(-jnp.expm1(-dt0))
    c_a_log = jnp.log(jax.random.uniform(next(keys), (N_ODD, H_C), f32, 1.0, 16.0))
    c_d_skip = gain((N_ODD, H_C))
    c_norm_g = gain((N_ODD, D_INNER_C))
    w_out_odd = nrm((N_ODD, D_INNER_C + W_D, D_MODEL), BETA * (D_INNER_C + W_D) ** -0.5)
    ln1_g = gain((DEPTH, D_MODEL))
    ln1_b = nrm((DEPTH, D_MODEL), 0.02)
    ln2_g = gain((DEPTH, D_MODEL))
    ln2_b = nrm((DEPTH, D_MODEL), 0.02)
    m_wq = nrm((DEPTH, D_MODEL, H_M * DH_M), D_MODEL ** -0.5)
    m_wkv = nrm((DEPTH, D_MODEL, 2 * H_M * DH_M), D_MODEL ** -0.5)
    m_wo = nrm((DEPTH, H_M * DH_M, D_MODEL), BETA * (H_M * DH_M) ** -0.5)
    return {'x_prompt': x_prompt, 'x_sample': x_sample, 'state_a': state_a,
            'cache_b1': cache_b1, 'cache_b2': cache_b2, 'cache_b3': cache_b3,
            'state_c_ssm': state_c_ssm, 'state_c_conv': state_c_conv, 'state_d': state_d,
            'cache_mem_kv': cache_mem_kv, 'mem_prompt': mem_prompt,
            'w_in_even': w_in_even, 'a_lb_logits': a_lb_logits, 'a_norm_g': a_norm_g, 'w_out_even': w_out_even,
            'w_in_odd': w_in_odd, 'c_conv_w': c_conv_w, 'c_conv_b': c_conv_b, 'c_dt_bias': c_dt_bias,
            'c_a_log': c_a_log, 'c_d_skip': c_d_skip, 'c_norm_g': c_norm_g, 'w_out_odd': w_out_odd,
            'ln1_g': ln1_g, 'ln1_b': ln1_b, 'ln2_g': ln2_g, 'ln2_b': ln2_b,
            'm_wq': m_wq, 'm_wkv': m_wkv, 'm_wo': m_wo}


def reference(x_prompt, x_sample, state_a, cache_b1, cache_b2, cache_b3, state_c_ssm, state_c_conv, state_d,
              cache_mem_kv, mem_prompt, w_in_even, a_lb_logits, a_norm_g, w_out_even, w_in_odd, c_conv_w, c_conv_b,
              c_dt_bias, c_a_log, c_d_skip, c_norm_g, w_out_odd, ln1_g, ln1_b, ln2_g, ln2_b, m_wq, m_wkv, m_wo):
    f32 = jnp.float32
    bp, tp = x_prompt.shape[0], x_prompt.shape[1]
    ts = x_sample.shape[1]
    pos_p = jnp.arange(tp, dtype=jnp.int32)
    pos_s = PAST_LEN + jnp.arange(ts, dtype=jnp.int32)
    sm = jax.nn.softmax(a_lb_logits.astype(f32), axis=0)
    lb_all = jnp.cumsum(sm, axis=0) - sm[0]
    xp, xs = x_prompt, x_sample
    a_p, a_s, c_p, c_s, cv_p, cv_s, d_p, d_s, mem_p = [], [], [], [], [], [], [], [], []
    b_p = [[] for _ in range(N_BG)]
    b_s = [[] for _ in range(N_BG)]
    for l in range(DEPTH):
        if l % 2 == 0:
            e = l // 2
            zeros_a = jnp.zeros((bp, H_A, DK_A, DV_A), f32)
            hp, sap, rp = even_mixer(xp, pos_p, lb_all[e], w_in_even[e], a_norm_g[e], w_out_even[e], zeros_a, None)
            hs, sas, rs = even_mixer(xs, pos_s, lb_all[e], w_in_even[e], a_norm_g[e], w_out_even[e], state_a[e],
                                     (cache_b1[e], cache_b2[e], cache_b3[e]))
            a_p.append(sap)
            a_s.append(sas)
            for g in range(N_BG):
                b_p[g].append(rp[g])
                b_s[g].append(rs[g])
        else:
            o = l // 2
            zeros_c = jnp.zeros((bp, H_C, N_C, P_C), f32)
            zeros_conv = jnp.zeros((bp, CONV_W - 1, CONV_DIM_C), xp.dtype)
            zeros_d = jnp.zeros((bp, H_D, DK_D, DV_D), f32)
            hp, scp, cvp, sdp = odd_mixer(xp, pos_p, w_in_odd[o], c_conv_w[o], c_conv_b[o], c_dt_bias[o], c_a_log[o],
                                          c_d_skip[o], c_norm_g[o], w_out_odd[o], zeros_c, zeros_conv, zeros_d)
            hs, scs, cvs, sds = odd_mixer(xs, pos_s, w_in_odd[o], c_conv_w[o], c_conv_b[o], c_dt_bias[o], c_a_log[o],
                                          c_d_skip[o], c_norm_g[o], w_out_odd[o], state_c_ssm[o], state_c_conv[o], state_d[o])
            c_p.append(scp)
            c_s.append(scs)
            cv_p.append(cvp)
            cv_s.append(cvs)
            d_p.append(sdp)
            d_s.append(sds)
        xp = layer_norm(ALPHA * xp + hp, ln1_g[l], ln1_b[l])
        xs = layer_norm(ALPHA * xs + hs, ln1_g[l], ln1_b[l])
        kv_p = memory_kv(mem_prompt, m_wkv[l])
        mem_p.append(kv_p)
        xp = layer_norm(ALPHA * xp + memory_attend(xp, kv_p, m_wq[l], m_wo[l]), ln2_g[l], ln2_b[l])
        xs = layer_norm(ALPHA * xs + memory_attend(xs, cache_mem_kv[l], m_wq[l], m_wo[l]), ln2_g[l], ln2_b[l])
    return (xp, xs, jnp.stack(a_p), jnp.stack(a_s),
            jnp.stack(b_p[0]), jnp.stack(b_p[1]), jnp.stack(b_p[2]),
            jnp.stack(b_s[0]), jnp.stack(b_s[1]), jnp.stack(b_s[2]),
            jnp.stack(c_p), jnp.stack(c_s), jnp.stack(cv_p), jnp.stack(cv_s),
            jnp.stack(d_p), jnp.stack(d_s), jnp.stack(mem_p))
```

```python
import functools
import math

import numpy as np
import jax
import jax.numpy as jnp
from jax import lax
from jax.experimental import pallas as pl
from jax.experimental.pallas import tpu as pltpu

F32 = jnp.float32
BF16 = jnp.bfloat16
HIGHEST = lax.Precision.HIGHEST

D_MODEL = 1024
DEPTH = 4
PAST_LEN = 2048
H_A, DK_A, DV_A = 4, 128, 128
W_A = H_A * DV_A
GLA_CHUNK = 64
GLA_SUB = 16
B_WINDOWS = (128, 512, 2048)
B_DILATIONS = (1, 4, 16)
N_BG, H_B, DH_B = 3, 8, 64
W_B = H_B * DH_B
SPAN = 128
ROT_DIM_B = DH_B // 4
ROPE_THETA = 500000.0
H_C, P_C, G_C, N_C, CONV_W = 16, 64, 2, 128, 4
D_INNER_C = H_C * P_C
CONV_DIM_C = D_INNER_C + 2 * G_C * N_C
SSD_CHUNK = 128
H_D, DK_D, DV_D = 4, 64, 128
W_D = H_D * DV_D
RET_CHUNK = 128
RET_THETA = 10000.0
H_M = 4
DH_M = D_MODEL // H_M
N_MEM = 256

ALPHA = (2.0 * DEPTH) ** 0.25
LN_EPS = 1e-5
RMS_EPS = 1e-6
MASK_NEG = -1e30

LANES = 128
EVEN_N = 7168
ODD_MAIN = 4096
ODD_N = ODD_MAIN + LANES
VMEM_LIMIT = 56 * 1024 * 1024


def _params(*sem):
    return pltpu.CompilerParams(dimension_semantics=sem, vmem_limit_bytes=VMEM_LIMIT)


def _nt(a, b):
    return lax.dot_general(a, b, (((1,), (1,)), ((), ())), preferred_element_type=F32)


def _tn(a, b):
    return lax.dot_general(a, b, (((0,), (0,)), ((), ())), preferred_element_type=F32)


def _dot(a, b):
    return jnp.dot(a, b, preferred_element_type=F32)


def _dot_exact(a, b):
    return jnp.dot(a, b, preferred_element_type=F32, precision=HIGHEST)


def _silu(x):
    return x * jax.nn.sigmoid(x)


def _iota(shape, dim):
    return lax.broadcasted_iota(jnp.int32, shape, dim)


def _rope(x, cf, sa, sb, shift):
    outs = []
    for j in range(x.shape[1] // LANES):
        xj = x[:, j * LANES:(j + 1) * LANES]
        outs.append(xj * cf + pltpu.roll(xj, shift, 1) * sa + pltpu.roll(xj, LANES - shift, 1) * sb)
    return outs[0] if len(outs) == 1 else jnp.concatenate(outs, axis=1)


def _rope_tables(pos, rot_dim, head_dim, theta):
    half = rot_dim // 2
    inv_freq = theta ** (-jnp.arange(half, dtype=F32) / half)
    ang = pos.astype(F32)[:, None] * inv_freq
    cos, sin = jnp.cos(ang), jnp.sin(ang)
    lane = np.arange(LANES) % head_dim
    idx = lane % half
    cf = jnp.where(lane < rot_dim, cos[:, idx], 1.0)
    sa = jnp.where((lane >= half) & (lane < rot_dim), sin[:, idx], 0.0)
    sb = jnp.where(lane < half, -sin[:, idx], 0.0)
    return cf.astype(F32), sa.astype(F32), sb.astype(F32)


def _mm_kernel(x_ref, w_ref, o_ref):
    o_ref[...] = _dot(x_ref[...], w_ref[...])


def matmul(x, w, tn):
    m, k = x.shape
    n = w.shape[1]
    tm = min(m, 1024)
    return pl.pallas_call(
        _mm_kernel,
        grid=(m // tm, n // tn),
        in_specs=[pl.BlockSpec((tm, k), lambda i, j: (i, 0)), pl.BlockSpec((k, tn), lambda i, j: (0, j))],
        out_specs=pl.BlockSpec((tm, tn), lambda i, j: (i, j)),
        out_shape=jax.ShapeDtypeStruct((m, n), F32),
        compiler_params=_params("parallel", "parallel"),
    )(x, w)


def _outproj_ln_kernel(*refs, n_parts):
    parts, ws = refs[:n_parts], refs[n_parts:2 * n_parts]
    x_ref, g_ref, b_ref, o_ref, ob_ref = refs[2 * n_parts:]
    acc = ALPHA * x_ref[...]
    for p_ref, w_ref in zip(parts, ws):
        acc = acc + _dot(p_ref[...], w_ref[...])
    mu = jnp.mean(acc, -1, keepdims=True)
    cen = acc - mu
    var = jnp.mean(cen * cen, -1, keepdims=True)
    y = cen * lax.rsqrt(var + LN_EPS) * g_ref[...] + b_ref[...]
    o_ref[...] = y
    ob_ref[...] = y.astype(BF16)


def outproj_ln(parts, ws, x, g, b):
    m = x.shape[0]
    tm = min(m, 512)
    row = lambda w: pl.BlockSpec((tm, w), lambda i: (i, 0))
    const = lambda a: pl.BlockSpec(a.shape, lambda i: (0, 0))
    return pl.pallas_call(
        functools.partial(_outproj_ln_kernel, n_parts=len(parts)),
        grid=(m // tm,),
        in_specs=[row(p.shape[1]) for p in parts] + [const(w) for w in ws] + [row(D_MODEL), const(g), const(b)],
        out_specs=[row(D_MODEL), row(D_MODEL)],
        out_shape=[jax.ShapeDtypeStruct((m, D_MODEL), F32), jax.ShapeDtypeStruct((m, D_MODEL), BF16)],
        compiler_params=_params("parallel"),
    )(*parts, *ws, x, g, b)


def _hgrn2_kernel(q_ref, f_ref, v_ref, g_ref, lb_ref, gn_ref, s0_ref, o_ref, sfin_ref, st_ref, acc_ref, *, chunk, sub):
    c = pl.program_id(1)

    @pl.when(c == 0)
    def _():
        for h in range(H_A):
            st_ref[h] = s0_ref[0, h].T

    lb = lb_ref[...]
    fx = f_ref[...]
    qx = q_ref[...]
    v = v_ref[...].astype(BF16)
    log_f = jnp.log(lb + (1.0 - lb) * jax.nn.sigmoid(fx))
    k = (1.0 - lb) * jax.nn.sigmoid(-fx)
    q = _silu(qx)
    tril = (_iota((chunk, chunk), 0) >= _iota((chunk, chunk), 1)).astype(F32)
    b = _dot_exact(tril, log_f)
    head = lambda a, h: a[:, h * DK_A:(h + 1) * DK_A]

    q_in = (q * jnp.exp(b)).astype(BF16)
    for h in range(H_A):
        acc_ref[:, h * DV_A:(h + 1) * DV_A] = _nt(head(q_in, h), st_ref[h].astype(BF16))

    for i in range(chunk // sub):
        r0, r1 = i * sub, (i + 1) * sub
        ref = b[r0 - 1:r0, :] if i > 0 else jnp.zeros((1, H_A * DK_A), F32)
        qt = (q[r0:r1] * jnp.exp(b[r0:r1] - ref)).astype(BF16)
        kt = (k[:r1] * jnp.exp(ref - b[:r1])).astype(BF16)
        causal = _iota((sub, r1), 1) <= _iota((sub, r1), 0) + r0
        for h in range(H_A):
            att = jnp.where(causal, _nt(head(qt, h), head(kt, h)), 0.0)
            acc_ref[r0:r1, h * DV_A:(h + 1) * DV_A] += _dot(att.astype(BF16), v[:r1, h * DV_A:(h + 1) * DV_A])

    b_last = b[chunk - 1:chunk, :]
    kw = (k * jnp.exp(b_last - b)).astype(BF16)
    g_last = jnp.exp(b_last)
    for h in range(H_A):
        st_ref[h] = st_ref[h] * head(g_last, h) + _tn(v[:, h * DV_A:(h + 1) * DV_A], head(kw, h))

    gate = _silu(g_ref[...])
    gn = gn_ref[...]
    for h in range(H_A):
        oh = acc_ref[:, h * DV_A:(h + 1) * DV_A]
        normed = oh * lax.rsqrt(jnp.mean(oh * oh, -1, keepdims=True) + RMS_EPS) * head(gn, h)
        o_ref[:, h * DV_A:(h + 1) * DV_A] = (normed * head(gate, h)).astype(BF16)

    @pl.when(c == pl.num_programs(1) - 1)
    def _():
        for h in range(H_A):
            sfin_ref[0, h] = st_ref[h].T


def hgrn2(proj, lb, gn, s0_all, e, bsz, t_len, chunk, sub):
    nc = t_len // chunk
    col = lambda j: pl.BlockSpec((chunk, W_A), lambda b, c, j=j: (b * nc + c, j))
    const = pl.BlockSpec((1, W_A), lambda b, c: (0, 0))
    state = pl.BlockSpec((None, 1, H_A, DK_A, DV_A), lambda b, c: (e, b, 0, 0, 0))
    return pl.pallas_call(
        functools.partial(_hgrn2_kernel, chunk=chunk, sub=sub),
        grid=(bsz, nc),
        in_specs=[col(0), col(1), col(2), col(3), const, const, state],
        out_specs=[pl.BlockSpec((chunk, W_A), lambda b, c: (b * nc + c, 0)),
                   pl.BlockSpec((1, H_A, DK_A, DV_A), lambda b, c: (b, 0, 0, 0))],
        out_shape=[jax.ShapeDtypeStruct((bsz * t_len, W_A), BF16),
                   jax.ShapeDtypeStruct((bsz, H_A, DK_A, DV_A), F32)],
        scratch_shapes=[pltpu.VMEM((H_A, DV_A, DK_A), F32), pltpu.VMEM((chunk, W_A), F32)],
        compiler_params=_params("parallel", "arbitrary"),
    )(proj, proj, proj, proj, lb, gn, s0_all)


def _band_attn_kernel(q_ref, k_ref, v_ref, cf_ref, sa_ref, sb_ref, o_ref, l_ref, kprev_ref, vprev_ref):
    n = pl.program_id(2)
    half = ROT_DIM_B // 2
    cf, sa, sb = cf_ref[...], sa_ref[...], sb_ref[...]
    q = _rope(q_ref[...], cf, sa, sb, half)
    kb = _rope(k_ref[...], cf, sa, sb, half).astype(BF16)
    vb = v_ref[...].astype(BF16)

    @pl.when(n == 0)
    def _():
        kprev_ref[...] = jnp.zeros_like(kprev_ref)
        vprev_ref[...] = jnp.zeros_like(vprev_ref)

    kk = jnp.concatenate([kprev_ref[...], kb], axis=0)
    vv = jnp.concatenate([vprev_ref[...], vb], axis=0)
    i = _iota((SPAN, 2 * SPAN), 0)
    j = _iota((SPAN, 2 * SPAN), 1)
    dist = i + SPAN - j
    valid = (dist >= 0) & (dist <= SPAN) & ((j >= SPAN) | (n > 0))
    low = _iota((SPAN, LANES), 1) < DH_B
    for hp in range(H_B // 2):
        sl = slice(hp * LANES, (hp + 1) * LANES)
        qp, kp, vp = q[:, sl], kk[:, sl], vv[:, sl]
        res = []
        for sel in (low, ~low):
            s = _nt(jnp.where(sel, qp, 0.0).astype(BF16), kp) * DH_B ** -0.5
            s = jnp.where(valid, s, MASK_NEG)
            m = jnp.max(s, -1, keepdims=True)
            p = jnp.exp(s - m)
            den = jnp.sum(p, -1, keepdims=True)
            res.append((_dot(p.astype(BF16), vp) / den, m + jnp.log(den)))
        o_ref[:, sl] = jnp.where(low, res[0][0], res[1][0])
        l_ref[:, sl] = jnp.where(low, res[0][1], res[1][1])
    kprev_ref[...] = kb
    vprev_ref[...] = vb


def band_attn(proj, tables, g, bsz, t_len):
    d = B_DILATIONS[g]
    nb = t_len // (d * SPAN)
    rows = bsz * t_len // d
    ncol = EVEN_N // W_B
    pr = proj.reshape(rows, d * EVEN_N)
    col = lambda j: pl.BlockSpec((SPAN, W_B), lambda b, r, n, j=j: (b * nb + n, r * ncol + j))
    tab = pl.BlockSpec((SPAN, LANES), lambda b, r, n: (n, r))
    out = pl.BlockSpec((SPAN, W_B), lambda b, r, n: (b * nb + n, r))
    tabs = [t.reshape(t_len // d, d * LANES) for t in tables]
    o, l = pl.pallas_call(
        _band_attn_kernel,
        grid=(bsz, d, nb),
        in_specs=[col(4 + g), col(7 + g), col(10 + g), tab, tab, tab],
        out_specs=[out, out],
        out_shape=[jax.ShapeDtypeStruct((rows, d * W_B), F32)] * 2,
        scratch_shapes=[pltpu.VMEM((SPAN, W_B), BF16), pltpu.VMEM((SPAN, W_B), BF16)],
        compiler_params=_params("parallel", "parallel", "arbitrary"),
    )(pr, pr, pr, *tabs)
    return o.reshape(bsz * t_len, W_B), l.reshape(bsz * t_len, W_B)


def _merge_kernel(o1, o2, o3, l1, l2, l3, g_ref, o_ref):
    la, lb_, lc = l1[...], l2[...], l3[...]
    m = jnp.maximum(jnp.maximum(la, lb_), lc)
    wa, wb, wc = jnp.exp(la - m), jnp.exp(lb_ - m), jnp.exp(lc - m)
    merged = (wa * o1[...] + wb * o2[...] + wc * o3[...]) / (wa + wb + wc)
    o_ref[...] = (merged * _silu(g_ref[...])).astype(BF16)


def merge_groups(outs, lses, proj):
    m = proj.shape[0]
    tm = 512
    row = pl.BlockSpec((tm, W_B), lambda i: (i, 0))
    return pl.pallas_call(
        _merge_kernel,
        grid=(m // tm,),
        in_specs=[row] * 6 + [pl.BlockSpec((tm, W_B), lambda i: (i, EVEN_N // W_B - 1))],
        out_specs=row,
        out_shape=jax.ShapeDtypeStruct((m, W_B), BF16),
        compiler_params=_params("parallel"),
    )(*outs, *lses, proj)


def _window_rows_kernel(k_ref, v_ref, cf_ref, sa_ref, sb_ref, o_ref):
    o_ref[0, :, :W_B] = _rope(k_ref[...], cf_ref[...], sa_ref[...], sb_ref[...], ROT_DIM_B // 2)
    o_ref[0, :, W_B:] = v_ref[...]


def window_rows(proj, tables, g, bsz, t_len):
    keep = min(B_WINDOWS[g], t_len)
    nk, nt = keep // SPAN, t_len // SPAN
    col = lambda j: pl.BlockSpec((SPAN, W_B), lambda b, i, j=j: (b * nt + nt - nk + i, j))
    tab = pl.BlockSpec((SPAN, LANES), lambda b, i: (nt - nk + i, 0))
    return pl.pallas_call(
        _window_rows_kernel,
        grid=(bsz, nk),
        in_specs=[col(7 + g), col(10 + g), tab, tab, tab],
        out_specs=pl.BlockSpec((1, SPAN, 2 * W_B), lambda b, i: (b, i, 0)),
        out_shape=jax.ShapeDtypeStruct((bsz, keep, 2 * W_B), F32),
        compiler_params=_params("parallel", "parallel"),
    )(proj, proj, *tables)


def _step_attn_kernel(p_ref, c1_ref, c2_ref, c3_ref, cf_ref, sa_ref, sb_ref,
                      o_ref, r1_ref, r2_ref, r3_ref, kall_ref, vall_ref, *, t_len):
    half = ROT_DIM_B // 2
    cf, sa, sb = cf_ref[...], sa_ref[...], sb_ref[...]
    nrow = H_B * t_len
    row = _iota((nrow, W_B), 0)
    own_head = (_iota((nrow, W_B), 1) // DH_B) == (row // t_len)
    outs, lses = [], []
    for g, (c_ref, r_ref) in enumerate(((c1_ref, r1_ref), (c2_ref, r2_ref), (c3_ref, r3_ref))):
        d = B_DILATIONS[g]
        past = min(B_WINDOWS[g], PAST_LEN)
        q = _rope(p_ref[:, (4 + g) * W_B:(5 + g) * W_B], cf, sa, sb, half)
        k_new = _rope(p_ref[:, (7 + g) * W_B:(8 + g) * W_B], cf, sa, sb, half)
        v_new = p_ref[:, (10 + g) * W_B:(11 + g) * W_B]
        r_ref[0, :, :W_B] = k_new
        r_ref[0, :, W_B:] = v_new
        cache = c_ref[...]
        if cache.ndim == 3:
            kept = cache.shape[1]
            cache = cache.reshape(cache.shape[0] * kept, 2 * W_B)
        else:
            kept = d
        lc = cache.shape[0]
        pad = jnp.zeros((LANES - t_len, W_B), F32)
        kall_ref[:lc] = cache[:, :W_B].astype(BF16)
        vall_ref[:lc] = cache[:, W_B:].astype(BF16)
        kall_ref[lc:lc + LANES] = jnp.concatenate([k_new, pad], axis=0).astype(BF16)
        vall_ref[lc:lc + LANES] = jnp.concatenate([v_new, pad], axis=0).astype(BF16)
        ncol = lc + LANES
        qb = jnp.where(own_head, jnp.concatenate([q] * H_B, axis=0), 0.0).astype(BF16)
        s = _nt(qb, kall_ref[:ncol]) * DH_B ** -0.5
        col = _iota((nrow, ncol), 1)
        t = _iota((nrow, ncol), 0) % t_len
        pos = jnp.where(col < lc, (col // kept) * d + col % kept, past + col - lc)
        dist = past + t - pos
        valid = (dist >= 0) & (dist % d == 0) & (dist // d <= SPAN) & (col < lc + t_len)
        s = jnp.where(valid, s, MASK_NEG)
        m = jnp.max(s, -1, keepdims=True)
        p = jnp.exp(s - m)
        den = jnp.sum(p, -1, keepdims=True)
        outs.append(_dot(p.astype(BF16), vall_ref[:ncol]) / den)
        lses.append(m + jnp.log(den))
    m = jnp.maximum(jnp.maximum(lses[0], lses[1]), lses[2])
    ws = [jnp.exp(l - m) for l in lses]
    merged = (ws[0] * outs[0] + ws[1] * outs[1] + ws[2] * outs[2]) / (ws[0] + ws[1] + ws[2])
    merged = jnp.where(own_head, merged, 0.0).reshape(H_B, t_len, W_B).sum(axis=0)
    o_ref[...] = (merged * _silu(p_ref[:, 13 * W_B:14 * W_B])).astype(BF16)


def step_attn(proj, caches, tables, e, bsz, t_len):
    l1, l2, l3 = (c.shape[2] for c in caches)
    d3 = B_DILATIONS[2]
    c1 = caches[0].reshape(-1, bsz, l1, 2 * W_B)
    c2 = caches[1].reshape(-1, bsz, l2, 2 * W_B)
    c3 = caches[2].reshape(-1, bsz, l3 // d3, d3, 2 * W_B)
    tab = pl.BlockSpec((t_len, LANES), lambda b: (0, 0))
    rows = pl.BlockSpec((1, t_len, 2 * W_B), lambda b: (b, 0, 0))
    lmax = max(l1, l2, l3 // d3 * t_len) + LANES
    return pl.pallas_call(
        functools.partial(_step_attn_kernel, t_len=t_len),
        grid=(bsz,),
        in_specs=[pl.BlockSpec((t_len, EVEN_N), lambda b: (b, 0)),
                  pl.BlockSpec((None, None, l1, 2 * W_B), lambda b: (e, b, 0, 0)),
                  pl.BlockSpec((None, None, l2, 2 * W_B), lambda b: (e, b, 0, 0)),
                  pl.BlockSpec((None, None, l3 // d3, t_len, 2 * W_B), lambda b: (e, b, 0, 0, 0)),
                  tab, tab, tab],
        out_specs=[pl.BlockSpec((t_len, W_B), lambda b: (b, 0)), rows, rows, rows],
        out_shape=[jax.ShapeDtypeStruct((bsz * t_len, W_B), BF16)]
        + [jax.ShapeDtypeStruct((bsz, t_len, 2 * W_B), F32)] * 3,
        scratch_shapes=[pltpu.VMEM((lmax, W_B), BF16), pltpu.VMEM((lmax, W_B), BF16)],
        compiler_params=_params("parallel"),
    )(proj, c1, c2, c3, *tables)


def _ssd_kernel(z_ref, x_ref, bc_ref, dt_ref, dtt_ref, cw_ref, cb_ref, dtb_ref, dtbc_ref, nega_ref, negac_ref,
                dskip_ref, ng_ref, expand_ref, conv0_ref, s0_ref,
                y_ref, convn_ref, sfin_ref, ext_ref, st_ref, *, chunk):
    c = pl.program_id(1)
    tail = CONV_W - 1

    @pl.when(c == 0)
    def _():
        ext_ref[0:8] = jnp.concatenate([jnp.zeros((8 - tail, CONV_DIM_C), F32), conv0_ref[0]], axis=0)
        st_ref[...] = s0_ref[0]

    ext_ref[8:8 + chunk, :D_INNER_C] = x_ref[...]
    ext_ref[8:8 + chunk, D_INNER_C:] = bc_ref[...]
    cw = cw_ref[...]
    conv = cb_ref[...] + sum(cw[j:j + 1, :] * ext_ref[8 - tail + j:8 - tail + j + chunk, :] for j in range(CONV_W))
    xbc = _silu(conv)

    @pl.when(c == pl.num_programs(1) - 1)
    def _():
        convn_ref[0] = ext_ref[8 + chunk - tail:8 + chunk, :]

    ext_ref[0:8] = ext_ref[chunk:chunk + 8]
    xs = xbc[:, :D_INNER_C]
    gn = G_C * N_C
    bm = xbc[:, D_INNER_C:D_INNER_C + gn].astype(BF16)
    cm = xbc[:, D_INNER_C + gn:].astype(BF16)

    softplus = lambda a: jnp.maximum(a, 0.0) + jnp.log1p(jnp.exp(-jnp.abs(a)))
    dt = softplus(dt_ref[...] + dtb_ref[...])
    la = dt * nega_ref[...]
    la_t = softplus(dtt_ref[0, 0] + dtbc_ref[...]) * negac_ref[...]
    ii = _iota((chunk, chunk), 0)
    jj = _iota((chunk, chunk), 1)
    causal = ii >= jj
    b = _dot_exact(causal.astype(F32), la)
    b_t = _dot_exact(la_t, (ii <= jj).astype(F32))
    expand = expand_ref[...]
    b_x = _dot_exact(b, expand)
    dt_x = _dot_exact(dt, expand)
    b_last = b_x[chunk - 1:chunk, :]
    xdt = xs * dt_x
    xdt_b = xdt.astype(BF16)
    xdtw = (xdt * jnp.exp(b_last - b_x)).astype(BF16)
    e_b = jnp.exp(b_x)
    low = _iota((chunk, LANES), 1) < P_C
    hg = H_C // G_C
    wg = hg * P_C
    for g in range(G_C):
        cg, bg = cm[:, g * N_C:(g + 1) * N_C], bm[:, g * N_C:(g + 1) * N_C]
        gs = slice(g * wg, (g + 1) * wg)
        y_ref[:, gs] = _dot(cg, st_ref[:, gs].astype(BF16)) * e_b[:, gs]
        scores = _nt(cg, bg)
        for hp in range(hg // 2):
            sl = slice(g * wg + hp * LANES, g * wg + (hp + 1) * LANES)
            res = []
            for h in (g * hg + 2 * hp, g * hg + 2 * hp + 1):
                decay = jnp.exp(jnp.where(causal, b[:, h:h + 1] - b_t[h:h + 1, :], MASK_NEG))
                res.append(_dot((scores * decay).astype(BF16), xdt_b[:, sl]))
            y_ref[:, sl] += jnp.where(low, res[0], res[1])
        st_ref[:, gs] = st_ref[:, gs] * jnp.exp(b_last[:, gs]) + _tn(bg, xdtw[:, gs])

    y = (y_ref[...] + dskip_ref[...] * xs) * _silu(z_ref[...])
    for g in range(G_C):
        gs = slice(g * wg, (g + 1) * wg)
        yg = y[:, gs]
        y_ref[:, gs] = yg * lax.rsqrt(jnp.mean(yg * yg, -1, keepdims=True) + RMS_EPS) * ng_ref[:, gs]

    @pl.when(c == pl.num_programs(1) - 1)
    def _():
        sfin_ref[0] = st_ref[...]


def ssd(proj, dt_t, prm, conv0_all, s0, o, bsz, t_len, chunk):
    nc = t_len // chunk
    blk = lambda w, j: pl.BlockSpec((chunk, w), lambda b, c, j=j: (b * nc + c, j))
    const = lambda a: pl.BlockSpec(a.shape, lambda b, c: (0,) * a.ndim)
    consts = [prm["conv_w"], prm["conv_b"], prm["dt_bias"], prm["dt_bias_col"], prm["neg_a"], prm["neg_a_col"],
              prm["d_skip"], prm["norm_g"], prm["expand"]]
    return pl.pallas_call(
        functools.partial(_ssd_kernel, chunk=chunk),
        grid=(bsz, nc),
        in_specs=[blk(D_INNER_C, 0), blk(D_INNER_C, 1), blk(2 * G_C * N_C, 4), blk(LANES, ODD_MAIN // LANES),
                  pl.BlockSpec((1, 1, H_C, chunk), lambda b, c: (b, c, 0, 0))]
        + [const(a) for a in consts]
        + [pl.BlockSpec((None, 1, CONV_W - 1, CONV_DIM_C), lambda b, c: (o, b, 0, 0)),
           pl.BlockSpec((1, N_C, D_INNER_C), lambda b, c: (b, 0, 0))],
        out_specs=[pl.BlockSpec((chunk, D_INNER_C), lambda b, c: (b * nc + c, 0)),
                   pl.BlockSpec((1, CONV_W - 1, CONV_DIM_C), lambda b, c: (b, 0, 0)),
                   pl.BlockSpec((1, N_C, D_INNER_C), lambda b, c: (b, 0, 0))],
        out_shape=[jax.ShapeDtypeStruct((bsz * t_len, D_INNER_C), F32),
                   jax.ShapeDtypeStruct((bsz, CONV_W - 1, CONV_DIM_C), F32),
                   jax.ShapeDtypeStruct((bsz, N_C, D_INNER_C), F32)],
        scratch_shapes=[pltpu.VMEM((chunk + 8, CONV_DIM_C), F32), pltpu.VMEM((N_C, D_INNER_C), F32)],
        compiler_params=_params("parallel", "arbitrary"),
    )(proj, proj, proj, proj, dt_t, *consts, conv0_all, s0)


def _retention_kernel(q_ref, k_ref, v_ref, g_ref, cf_ref, sa_ref, sb_ref, dmat_ref, gq_ref, gw_ref, gs_ref, s0_ref,
                      o_ref, sfin_ref, st_ref, *, chunk):
    c = pl.program_id(1)

    @pl.when(c == 0)
    def _():
        st_ref[...] = s0_ref[0].reshape(H_D * DK_D, DV_D)

    half = DK_D // 2
    cf, sa, sb = cf_ref[...], sa_ref[...], sb_ref[...]
    qr = _rope(q_ref[...], cf, sa, sb, half)
    kr = _rope(k_ref[...], cf, sa, sb, half) * DK_D ** -0.5
    v = v_ref[...].astype(BF16)
    gate = _silu(g_ref[...])
    low = _iota((chunk, LANES), 1) < DK_D
    for hp in range(H_D // 2):
        sl = slice(hp * LANES, (hp + 1) * LANES)
        qp, kp = qr[:, sl], kr[:, sl]
        kp_b = kp.astype(BF16)
        s_pair = st_ref[sl, :]
        s_pair_b = s_pair.astype(BF16)
        upd = s_pair * gs_ref[hp]
        for idx, sel in enumerate((low, ~low)):
            h = 2 * hp + idx
            vh = v[:, h * DV_D:(h + 1) * DV_D]
            qm = jnp.where(sel, qp, 0.0).astype(BF16)
            att = (_nt(qm, kp_b) * dmat_ref[h]).astype(BF16)
            oh = _dot(att, vh) + _dot(qm, s_pair_b) * gq_ref[h]
            mu = jnp.mean(oh, -1, keepdims=True)
            cen = oh - mu
            var = jnp.mean(cen * cen, -1, keepdims=True)
            o_ref[:, h * DV_D:(h + 1) * DV_D] = cen * lax.rsqrt(var + LN_EPS) * gate[:, h * DV_D:(h + 1) * DV_D]
            upd = upd + _tn((jnp.where(sel, kp, 0.0) * gw_ref[h]).astype(BF16), vh)
        st_ref[sl, :] = upd

    @pl.when(c == pl.num_programs(1) - 1)
    def _():
        sfin_ref[0] = st_ref[...].reshape(H_D, DK_D, DV_D)


def retention(proj, tables, consts, s0_all, o, bsz, t_len, chunk):
    nc = t_len // chunk
    blk = lambda w, j: pl.BlockSpec((chunk, w), lambda b, c, j=j: (b * nc + c, j))
    tab = pl.BlockSpec((chunk, LANES), lambda b, c: (c, 0))
    const = lambda a: pl.BlockSpec(a.shape, lambda b, c: (0,) * a.ndim)
    wqk = H_D * DK_D
    return pl.pallas_call(
        functools.partial(_retention_kernel, chunk=chunk),
        grid=(bsz, nc),
        in_specs=[blk(wqk, 2560 // wqk), blk(wqk, 2816 // wqk), blk(W_D, 3072 // W_D), blk(W_D, 3584 // W_D),
                  tab, tab, tab] + [const(a) for a in consts]
        + [pl.BlockSpec((None, 1, H_D, DK_D, DV_D), lambda b, c: (o, b, 0, 0, 0))],
        out_specs=[pl.BlockSpec((chunk, W_D), lambda b, c: (b * nc + c, 0)),
                   pl.BlockSpec((1, H_D, DK_D, DV_D), lambda b, c: (b, 0, 0, 0))],
        out_shape=[jax.ShapeDtypeStruct((bsz * t_len, W_D), F32),
                   jax.ShapeDtypeStruct((bsz, H_D, DK_D, DV_D), F32)],
        scratch_shapes=[pltpu.VMEM((H_D * DK_D, DV_D), F32)],
        compiler_params=_params("parallel", "arbitrary"),
    )(proj, proj, proj, proj, *tables, *consts, s0_all)


def _retention_consts(chunk):
    lg = np.log1p(-np.exp2(-5.0 - np.arange(H_D, dtype=np.float64)))
    t = np.arange(chunk)
    diff = t[:, None] - t[None, :]
    dmat = np.where(diff >= 0, np.exp(lg[:, None, None] * diff), 0.0)
    gq = np.broadcast_to(np.exp(lg[:, None, None] * (t[None, :, None] + 1)), (H_D, chunk, DV_D))
    gw = np.broadcast_to(np.exp(lg[:, None, None] * (chunk - 1 - t[None, :, None])), (H_D, chunk, LANES))
    gs = np.repeat(np.exp(lg * chunk), DK_D).reshape(H_D // 2, 2 * DK_D, 1)
    gs = np.broadcast_to(gs, (H_D // 2, 2 * DK_D, DV_D))
    return [jnp.asarray(a, F32) for a in (dmat, gq, gw, gs)]


def _mem_attn_kernel(q_ref, kv_ref, o_ref):
    q = q_ref[...].astype(BF16)
    kv = kv_ref[...].astype(BF16)
    for h in range(H_M):
        sl = slice(h * DH_M, (h + 1) * DH_M)
        s = _nt(q[:, sl], kv[:, sl]) * DH_M ** -0.5
        m = jnp.max(s, -1, keepdims=True)
        p = jnp.exp(s - m)
        den = jnp.sum(p, -1, keepdims=True)
        o_ref[:, sl] = (_dot(p.astype(BF16), kv[:, D_MODEL + h * DH_M:D_MODEL + (h + 1) * DH_M]) / den).astype(BF16)


def mem_attn(q, kv, layer, bsz, t_len):
    tq = min(t_len, 512)
    nq = t_len // tq
    return pl.pallas_call(
        _mem_attn_kernel,
        grid=(bsz, nq),
        in_specs=[pl.BlockSpec((tq, D_MODEL), lambda b, i: (b * nq + i, 0)),
                  pl.BlockSpec((None, None, N_MEM, 2 * D_MODEL), lambda b, i: (layer, b, 0, 0))],
        out_specs=pl.BlockSpec((tq, D_MODEL), lambda b, i: (b * nq + i, 0)),
        out_shape=jax.ShapeDtypeStruct((bsz * t_len, D_MODEL), BF16),
        compiler_params=_params("parallel", "arbitrary"),
    )(q, kv)


def _even_layer(x, xb, e, w_in, w_out, lb, gn, ln_g, ln_b, state_a, bsz, t_len, tables, caches):
    proj = matmul(xb, w_in, 512)
    if caches is None:
        o_a, s_a = hgrn2(proj, lb, gn, state_a, 0, bsz, t_len, GLA_CHUNK, GLA_SUB)
        res = [band_attn(proj, tables, g, bsz, t_len) for g in range(N_BG)]
        o_b = merge_groups([r[0] for r in res], [r[1] for r in res], proj)
        rows = [window_rows(proj, tables, g, bsz, t_len) for g in range(N_BG)]
    else:
        o_a, s_a = hgrn2(proj, lb, gn, state_a, e, bsz, t_len, t_len, t_len)
        o_b, *rows = step_attn(proj, caches, tables, e, bsz, t_len)
    x, xb = outproj_ln([o_a, o_b], [w_out[:W_A], w_out[W_A:]], x, ln_g, ln_b)
    rows = [r.reshape(bsz, r.shape[1], 2, H_B, DH_B) for r in rows]
    return x, xb, s_a, rows


def _odd_layer(x, xb, o, w_in, w_out, prm, ln_g, ln_b, s_c0, conv0, s_d0, bsz, t_len, chunk, tables, ret_consts):
    proj = matmul(xb, w_in, 384)
    nc = t_len // chunk
    dt_t = proj[:, ODD_MAIN:ODD_MAIN + H_C].reshape(bsz, nc, chunk, H_C).transpose(0, 1, 3, 2)
    y_c, conv_n, s_c = ssd(proj, dt_t, prm, conv0, s_c0, o, bsz, t_len, chunk)
    o_d, s_d = retention(proj, tables, ret_consts, s_d0, o, bsz, t_len, chunk)
    x, xb = outproj_ln([y_c.astype(BF16), o_d.astype(BF16)], [w_out[:D_INNER_C], w_out[D_INNER_C:]], x, ln_g, ln_b)
    return x, xb, s_c, conv_n, s_d


def _memory_layer(x, xb, layer, wq, wo, kv, ln_g, ln_b, bsz, t_len):
    q = matmul(xb, wq, 512)
    att = mem_attn(q, kv, layer, bsz, t_len)
    return outproj_ln([att], [wo], x, ln_g, ln_b)


def _state_to_lanes(s):
    return s.transpose(0, 2, 1, 3).reshape(s.shape[0], N_C, D_INNER_C)


def _state_from_lanes(s):
    return s.reshape(s.shape[0], N_C, H_C, P_C).transpose(0, 2, 1, 3)


def kernel(x_prompt, x_sample, state_a, cache_b1, cache_b2, cache_b3, state_c_ssm, state_c_conv, state_d,
           cache_mem_kv, mem_prompt, w_in_even, a_lb_logits, a_norm_g, w_out_even, w_in_odd, c_conv_w, c_conv_b,
           c_dt_bias, c_a_log, c_d_skip, c_norm_g, w_out_odd, ln1_g, ln1_b, ln2_g, ln2_b, m_wq, m_wkv, m_wo):
    bp, tp = x_prompt.shape[0], x_prompt.shape[1]
    bs, ts = x_sample.shape[0], x_sample.shape[1]
    pos_p = jnp.arange(tp, dtype=jnp.int32)
    pos_s = PAST_LEN + jnp.arange(ts, dtype=jnp.int32)
    tab_b = {"p": _rope_tables(pos_p, ROT_DIM_B, DH_B, ROPE_THETA), "s": _rope_tables(pos_s, ROT_DIM_B, DH_B, ROPE_THETA)}
    tab_d = {"p": _rope_tables(pos_p, DK_D, DK_D, RET_THETA), "s": _rope_tables(pos_s, DK_D, DK_D, RET_THETA)}
    ret_consts = {"p": _retention_consts(RET_CHUNK), "s": _retention_consts(ts)}

    sm = jax.nn.softmax(a_lb_logits.astype(F32), axis=0)
    lb_all = jnp.cumsum(sm, axis=0) - sm[0]
    expand = jnp.asarray(np.pad(np.repeat(np.eye(H_C), P_C, axis=1), ((0, LANES - H_C), (0, 0))), F32)
    pad_h = lambda a: jnp.pad(a.astype(F32), (0, LANES - H_C))[None, :]
    dt0 = D_INNER_C + CONV_DIM_C
    zeros_a = jnp.zeros((1, bp, H_A, DK_A, DV_A), F32)
    zeros_c = jnp.zeros((bp, N_C, D_INNER_C), F32)
    zeros_conv = jnp.zeros((1, bp, CONV_W - 1, CONV_DIM_C), F32)
    zeros_d = jnp.zeros((1, bp, H_D, DK_D, DV_D), F32)
    caches = (cache_b1, cache_b2, cache_b3)
    mem_kv_s = cache_mem_kv.reshape(DEPTH, bs, N_MEM, 2 * D_MODEL)

    xp, xs = x_prompt.reshape(bp * tp, D_MODEL), x_sample.reshape(bs * ts, D_MODEL)
    xpb, xsb = xp.astype(BF16), xs.astype(BF16)
    mem_b = mem_prompt.reshape(bp * N_MEM, D_MODEL).astype(BF16)
    a_p, a_s, c_p, c_s, cv_p, cv_s, d_p, d_s, mem_p = [], [], [], [], [], [], [], [], []
    b_p = [[] for _ in range(N_BG)]
    b_s = [[] for _ in range(N_BG)]
    for l in range(DEPTH):
        if l % 2 == 0:
            e = l // 2
            w_in, w_out = w_in_even[e].astype(BF16), w_out_even[e].astype(BF16)
            lb = lb_all[e][None, :]
            gn = jnp.tile(a_norm_g[e].astype(F32), H_A)[None, :]
            g1, b1 = ln1_g[l][None, :], ln1_b[l][None, :]
            xp, xpb, sap, rp = _even_layer(xp, xpb, e, w_in, w_out, lb, gn, g1, b1, zeros_a, bp, tp, tab_b["p"], None)
            xs, xsb, sas, rs = _even_layer(xs, xsb, e, w_in, w_out, lb, gn, g1, b1, state_a, bs, ts, tab_b["s"], caches)
            a_p.append(sap)
            a_s.append(sas)
            for g in range(N_BG):
                b_p[g].append(rp[g])
                b_s[g].append(rs[g])
        else:
            o = l // 2
            w = w_in_odd[o]
            w_in = jnp.concatenate([w[:, :dt0], w[:, dt0 + H_C:], w[:, dt0:dt0 + H_C],
                                    jnp.zeros((D_MODEL, LANES - H_C), w.dtype)], axis=1).astype(BF16)
            w_out = w_out_odd[o].astype(BF16)
            prm = {
                "conv_w": c_conv_w[o].astype(F32), "conv_b": c_conv_b[o].astype(F32)[None, :],
                "dt_bias": pad_h(c_dt_bias[o]), "dt_bias_col": c_dt_bias[o].astype(F32)[:, None],
                "neg_a": pad_h(-jnp.exp(c_a_log[o].astype(F32))), "neg_a_col": -jnp.exp(c_a_log[o].astype(F32))[:, None],
                "d_skip": jnp.repeat(c_d_skip[o].astype(F32), P_C)[None, :], "norm_g": c_norm_g[o].astype(F32)[None, :],
                "expand": expand,
            }
            g1, b1 = ln1_g[l][None, :], ln1_b[l][None, :]
            xp, xpb, scp, cvp, sdp = _odd_layer(xp, xpb, 0, w_in, w_out, prm, g1, b1, zeros_c, zeros_conv, zeros_d,
                                                bp, tp, SSD_CHUNK, tab_d["p"], ret_consts["p"])
            xs, xsb, scs, cvs, sds = _odd_layer(xs, xsb, o, w_in, w_out, prm, g1, b1, _state_to_lanes(state_c_ssm[o]),
                                                state_c_conv, state_d, bs, ts, ts, tab_d["s"], ret_consts["s"])
            c_p.append(_state_from_lanes(scp))
            c_s.append(_state_from_lanes(scs))
            cv_p.append(cvp)
            cv_s.append(cvs)
            d_p.append(sdp)
            d_s.append(sds)
        kv_p = matmul(mem_b, m_wkv[l].astype(BF16), 512)
        mem_p.append(kv_p.reshape(bp, N_MEM, 2, H_M, DH_M))
        wq, wo = m_wq[l].astype(BF16), m_wo[l].astype(BF16)
        g2, b2 = ln2_g[l][None, :], ln2_b[l][None, :]
        xp, xpb = _memory_layer(xp, xpb, 0, wq, wo, kv_p.reshape(1, bp, N_MEM, 2 * D_MODEL), g2, b2, bp, tp)
        xs, xsb = _memory_layer(xs, xsb, l, wq, wo, mem_kv_s, g2, b2, bs, ts)
    return (xp.reshape(bp, tp, D_MODEL), xs.reshape(bs, ts, D_MODEL), jnp.stack(a_p), jnp.stack(a_s),
            jnp.stack(b_p[0]), jnp.stack(b_p[1]), jnp.stack(b_p[2]),
            jnp.stack(b_s[0]), jnp.stack(b_s[1]), jnp.stack(b_s[2]),
            jnp.stack(c_p), jnp.stack(c_s), jnp.stack(cv_p), jnp.stack(cv_s),
            jnp.stack(d_p), jnp.stack(d_s), jnp.stack(mem_p))
```

```python
import functools
import math

import numpy as np
import jax
import jax.numpy as jnp
from jax import lax
from jax.experimental import pallas as pl
from jax.experimental.pallas import tpu as pltpu

F32 = jnp.float32
BF16 = jnp.bfloat16
HIGHEST = lax.Precision.HIGHEST

D_MODEL = 1024
DEPTH = 4
PAST_LEN = 2048
H_A, DK_A, DV_A = 4, 128, 128
W_A = H_A * DV_A
GLA_CHUNK = 64
GLA_SUB = 16
GLA_ROWS = 256
B_WINDOWS = (128, 512, 2048)
B_DILATIONS = (1, 4, 16)
N_BG, H_B, DH_B = 3, 8, 64
W_B = H_B * DH_B
SPAN = 128
ROT_DIM_B = DH_B // 4
ROPE_THETA = 500000.0
H_C, P_C, G_C, N_C, CONV_W = 16, 64, 2, 128, 4
D_INNER_C = H_C * P_C
CONV_DIM_C = D_INNER_C + 2 * G_C * N_C
SSD_CHUNK = 128
H_D, DK_D, DV_D = 4, 64, 128
W_D = H_D * DV_D
RET_CHUNK = 128
RET_THETA = 10000.0
H_M = 4
DH_M = D_MODEL // H_M
N_MEM = 256

ALPHA = (2.0 * DEPTH) ** 0.25
LN_EPS = 1e-5
RMS_EPS = 1e-6
MASK_NEG = -1e30

LANES = 128
EVEN_N = 7168
ODD_MAIN = 4096
ODD_N = ODD_MAIN + LANES
VMEM_LIMIT = 56 * 1024 * 1024


def _params(*sem):
    return pltpu.CompilerParams(dimension_semantics=sem, vmem_limit_bytes=VMEM_LIMIT)


def _nt(a, b):
    return lax.dot_general(a, b, (((1,), (1,)), ((), ())), preferred_element_type=F32)


def _tn(a, b):
    return lax.dot_general(a, b, (((0,), (0,)), ((), ())), preferred_element_type=F32)


def _dot(a, b):
    return jnp.dot(a, b, preferred_element_type=F32)


def _dot_exact(a, b):
    return jnp.dot(a, b, preferred_element_type=F32, precision=HIGHEST)


def _silu(x):
    return x * jax.nn.sigmoid(x)


def _iota(shape, dim):
    return lax.broadcasted_iota(jnp.int32, shape, dim)


def _rope(x, cf, sa, sb, shift):
    outs = []
    for j in range(x.shape[1] // LANES):
        xj = x[:, j * LANES:(j + 1) * LANES]
        outs.append(xj * cf + pltpu.roll(xj, shift, 1) * sa + pltpu.roll(xj, LANES - shift, 1) * sb)
    return outs[0] if len(outs) == 1 else jnp.concatenate(outs, axis=1)


def _rope_tables(pos, rot_dim, head_dim, theta):
    half = rot_dim // 2
    inv_freq = theta ** (-jnp.arange(half, dtype=F32) / half)
    ang = pos.astype(F32)[:, None] * inv_freq
    cos, sin = jnp.cos(ang), jnp.sin(ang)
    lane = np.arange(LANES) % head_dim
    idx = lane % half
    cf = jnp.where(lane < rot_dim, cos[:, idx], 1.0)
    sa = jnp.where((lane >= half) & (lane < rot_dim), sin[:, idx], 0.0)
    sb = jnp.where(lane < half, -sin[:, idx], 0.0)
    return cf.astype(F32), sa.astype(F32), sb.astype(F32)


def _mm_kernel(x_ref, w_ref, o_ref):
    o_ref[...] = _dot(x_ref[...], w_ref[...])


def matmul(x, w, tn):
    m, k = x.shape
    n = w.shape[1]
    tm = min(m, 1024)
    return pl.pallas_call(
        _mm_kernel,
        grid=(m // tm, n // tn),
        in_specs=[pl.BlockSpec((tm, k), lambda i, j: (i, 0)), pl.BlockSpec((k, tn), lambda i, j: (0, j))],
        out_specs=pl.BlockSpec((tm, tn), lambda i, j: (i, j)),
        out_shape=jax.ShapeDtypeStruct((m, n), F32),
        compiler_params=_params("parallel", "parallel"),
        name="matmul",
    )(x, w)


def _outproj_ln_kernel(*refs, n_parts):
    parts, ws = refs[:n_parts], refs[n_parts:2 * n_parts]
    x_ref, g_ref, b_ref, o_ref, ob_ref = refs[2 * n_parts:]
    acc = ALPHA * x_ref[...]
    for p_ref, w_ref in zip(parts, ws):
        acc = acc + _dot(p_ref[...], w_ref[...])
    mu = jnp.mean(acc, -1, keepdims=True)
    cen = acc - mu
    var = jnp.mean(cen * cen, -1, keepdims=True)
    y = cen * lax.rsqrt(var + LN_EPS) * g_ref[...] + b_ref[...]
    o_ref[...] = y
    ob_ref[...] = y.astype(BF16)


def outproj_ln(parts, ws, x, g, b):
    m = x.shape[0]
    tm = min(m, 512)
    row = lambda w: pl.BlockSpec((tm, w), lambda i: (i, 0))
    const = lambda a: pl.BlockSpec(a.shape, lambda i: (0, 0))
    return pl.pallas_call(
        functools.partial(_outproj_ln_kernel, n_parts=len(parts)),
        grid=(m // tm,),
        in_specs=[row(p.shape[1]) for p in parts] + [const(w) for w in ws] + [row(D_MODEL), const(g), const(b)],
        out_specs=[row(D_MODEL), row(D_MODEL)],
        out_shape=[jax.ShapeDtypeStruct((m, D_MODEL), F32), jax.ShapeDtypeStruct((m, D_MODEL), BF16)],
        compiler_params=_params("parallel"),
        name="outproj_ln",
    )(*parts, *ws, x, g, b)


def _hgrn2_kernel(q_ref, f_ref, v_ref, g_ref, lb_ref, gn_ref, s0_ref, o_ref, sfin_ref, st_ref, *, chunk, sub):
    c = pl.program_id(1)

    @pl.when(c == 0)
    def _():
        for h in range(H_A):
            st_ref[h] = s0_ref[0, h].T

    lb = lb_ref[...]
    gn = gn_ref[...]
    starts = range(0, q_ref.shape[0], chunk)
    nsub = chunk // sub
    tril = (_iota((chunk, chunk), 0) >= _iota((chunk, chunk), 1)).astype(F32)
    head = lambda a, h: a[:, h * DK_A:(h + 1) * DK_A]
    lane_head = _iota((H_A * sub, W_A), 1) // DK_A == _iota((H_A * sub, W_A), 0) // sub

    fx = f_ref[...]
    log_f = jnp.log(lb + (1.0 - lb) * jax.nn.sigmoid(fx))
    k = (1.0 - lb) * jax.nn.sigmoid(-fx)
    q = _silu(q_ref[...])
    v = v_ref[...].astype(BF16)
    bs = [_dot_exact(tril, log_f[c0:c0 + chunk]) for c0 in starts]

    q_in, g_last, kw, q_bd, kt = [], [], [], [], []
    for c0, b in zip(starts, bs):
        qc, kc = q[c0:c0 + chunk], k[c0:c0 + chunk]
        b_last = b[chunk - 1:chunk, :]
        q_in.append((qc * jnp.exp(b)).astype(BF16))
        g_last.append(jnp.exp(b_last))
        kw.append((kc * jnp.exp(b_last - b)).astype(BF16))
        for i in range(nsub):
            r0, r1 = i * sub, (i + 1) * sub
            ref = b[r0 - 1:r0, :] if i > 0 else jnp.zeros((1, W_A), F32)
            qt = qc[r0:r1] * jnp.exp(b[r0:r1] - ref)
            q_bd.append(jnp.where(lane_head, jnp.concatenate([qt] * H_A, axis=0), 0.0).astype(BF16))
            kt.append((kc[:r1] * jnp.exp(ref - b[:r1])).astype(BF16))
    upd = [[_tn(v[c0:c0 + chunk, h * DV_A:(h + 1) * DV_A], head(kw_c, h)) for h in range(H_A)]
           for c0, kw_c in zip(starts, kw)]
    scores = [_nt(a, b_) for a, b_ in zip(q_bd, kt)]
    atts = []
    for idx, s in enumerate(scores):
        r0 = (idx % nsub) * sub
        causal = _iota(s.shape, 1) <= _iota(s.shape, 0) % sub + r0
        atts.append(jnp.where(causal, s, 0.0).astype(BF16))
    intra = []
    for idx, att in enumerate(atts):
        c0 = starts[idx // nsub]
        pv = _dot(att, v[c0:c0 + att.shape[1]])
        intra.append([pv[h * sub:(h + 1) * sub, h * DV_A:(h + 1) * DV_A] for h in range(H_A)])

    gate = _silu(g_ref[...])
    state = [st_ref[h] for h in range(H_A)]
    for ci, c0 in enumerate(starts):
        for h in range(H_A):
            oh = _nt(head(q_in[ci], h), state[h].astype(BF16))
            oh = oh + jnp.concatenate([intra[ci * nsub + i][h] for i in range(nsub)], axis=0)
            normed = oh * lax.rsqrt(jnp.mean(oh * oh, -1, keepdims=True) + RMS_EPS) * head(gn, h)
            o_ref[c0:c0 + chunk, h * DV_A:(h + 1) * DV_A] = (normed * head(gate[c0:c0 + chunk], h)).astype(BF16)
            state[h] = state[h] * head(g_last[ci], h) + upd[ci][h]
    for h in range(H_A):
        st_ref[h] = state[h]

    @pl.when(c == pl.num_programs(1) - 1)
    def _():
        for h in range(H_A):
            sfin_ref[0, h] = st_ref[h].T


def hgrn2(proj, lb, gn, s0_all, e, bsz, t_len, chunk, sub, rows):
    nc = t_len // rows
    col = lambda j: pl.BlockSpec((rows, W_A), lambda b, c, j=j: (b * nc + c, j))
    const = pl.BlockSpec((1, W_A), lambda b, c: (0, 0))
    state = pl.BlockSpec((None, 1, H_A, DK_A, DV_A), lambda b, c: (e, b, 0, 0, 0))
    return pl.pallas_call(
        functools.partial(_hgrn2_kernel, chunk=chunk, sub=sub),
        grid=(bsz, nc),
        in_specs=[col(0), col(1), col(2), col(3), const, const, state],
        out_specs=[pl.BlockSpec((rows, W_A), lambda b, c: (b * nc + c, 0)),
                   pl.BlockSpec((1, H_A, DK_A, DV_A), lambda b, c: (b, 0, 0, 0))],
        out_shape=[jax.ShapeDtypeStruct((bsz * t_len, W_A), BF16),
                   jax.ShapeDtypeStruct((bsz, H_A, DK_A, DV_A), F32)],
        scratch_shapes=[pltpu.VMEM((H_A, DV_A, DK_A), F32)],
        compiler_params=_params("parallel", "arbitrary"),
        name="hgrn2",
    )(proj, proj, proj, proj, lb, gn, s0_all)


BAND_TILE = SPAN * max(B_DILATIONS)


def _band_attn_kernel(q_ref, k_ref, v_ref, cf_ref, sa_ref, sb_ref, o_ref, l_ref, kprev_ref, vprev_ref, *, d):
    n = pl.program_id(2)
    half = ROT_DIM_B // 2
    nblk = BAND_TILE // (SPAN * d)

    @pl.when(n == 0)
    def _():
        kprev_ref[...] = jnp.zeros_like(kprev_ref)
        vprev_ref[...] = jnp.zeros_like(vprev_ref)

    i = _iota((SPAN, 2 * SPAN), 0)
    j = _iota((SPAN, 2 * SPAN), 1)
    dist = i + SPAN - j
    band = (dist >= 0) & (dist <= SPAN)
    low = _iota((SPAN, LANES), 1) < DH_B

    def block(idx, carry):
        r, jb = idx // nblk, idx % nblk
        start = jb * (SPAN * d) + r
        rows = pl.ds(start, SPAN, stride=d) if d > 1 else pl.ds(pl.multiple_of(start, SPAN), SPAN)
        cf, sa, sb = cf_ref[rows, :], sa_ref[rows, :], sb_ref[rows, :]
        q = _rope(q_ref[rows, :], cf, sa, sb, half)
        kb = _rope(k_ref[rows, :], cf, sa, sb, half).astype(BF16)
        vb = v_ref[rows, :].astype(BF16)
        kk = jnp.concatenate([kprev_ref[r], kb], axis=0)
        vv = jnp.concatenate([vprev_ref[r], vb], axis=0)
        valid = band & ((j >= SPAN) | (n > 0) | (jb > 0))
        res = []
        for sel in (low, ~low):
            s = _nt(jnp.where(sel, q, 0.0).astype(BF16), kk) * DH_B ** -0.5
            s = jnp.where(valid, s, MASK_NEG)
            m = jnp.max(s, -1, keepdims=True)
            p = jnp.exp(s - m)
            den = jnp.sum(p, -1, keepdims=True)
            res.append((_dot(p.astype(BF16), vv) / den, m + jnp.log(den)))
        o_ref[rows, :] = jnp.where(low, res[0][0], res[1][0])
        l_ref[rows, :] = jnp.where(low, res[0][1], res[1][1])
        kprev_ref[r] = kb
        vprev_ref[r] = vb
        return carry

    lax.fori_loop(0, d * nblk, block, 0)


def band_attn(proj, tables, g, bsz, t_len):
    d = B_DILATIONS[g]
    nt = t_len // BAND_TILE
    npair = W_B // LANES
    col = lambda j: pl.BlockSpec((BAND_TILE, LANES), lambda b, hp, n, j=j: (b * nt + n, j * npair + hp))
    tab = pl.BlockSpec((BAND_TILE, LANES), lambda b, hp, n: (n, 0))
    out = pl.BlockSpec((BAND_TILE, LANES), lambda b, hp, n: (b * nt + n, hp))
    return pl.pallas_call(
        functools.partial(_band_attn_kernel, d=d),
        grid=(bsz, npair, nt),
        in_specs=[col(4 + g), col(7 + g), col(10 + g), tab, tab, tab],
        out_specs=[out, out],
        out_shape=[jax.ShapeDtypeStruct((bsz * t_len, W_B), F32)] * 2,
        scratch_shapes=[pltpu.VMEM((d, SPAN, LANES), BF16), pltpu.VMEM((d, SPAN, LANES), BF16)],
        compiler_params=_params("parallel", "parallel", "arbitrary"),
        name=f"band_attn_d{d}",
    )(proj, proj, proj, *tables)


def _merge_kernel(o1, o2, o3, l1, l2, l3, g_ref, o_ref):
    la, lb_, lc = l1[...], l2[...], l3[...]
    m = jnp.maximum(jnp.maximum(la, lb_), lc)
    wa, wb, wc = jnp.exp(la - m), jnp.exp(lb_ - m), jnp.exp(lc - m)
    merged = (wa * o1[...] + wb * o2[...] + wc * o3[...]) / (wa + wb + wc)
    o_ref[...] = (merged * _silu(g_ref[...])).astype(BF16)


def merge_groups(outs, lses, proj):
    m = proj.shape[0]
    tm = 512
    row = pl.BlockSpec((tm, W_B), lambda i: (i, 0))
    return pl.pallas_call(
        _merge_kernel,
        grid=(m // tm,),
        in_specs=[row] * 6 + [pl.BlockSpec((tm, W_B), lambda i: (i, EVEN_N // W_B - 1))],
        out_specs=row,
        out_shape=jax.ShapeDtypeStruct((m, W_B), BF16),
        compiler_params=_params("parallel"),
        name="merge_groups",
    )(*outs, *lses, proj)


def _window_rows_kernel(k_ref, v_ref, cf_ref, sa_ref, sb_ref, o_ref):
    o_ref[0, :, :W_B] = _rope(k_ref[...], cf_ref[...], sa_ref[...], sb_ref[...], ROT_DIM_B // 2)
    o_ref[0, :, W_B:] = v_ref[...]


def window_rows(proj, tables, g, bsz, t_len):
    keep = min(B_WINDOWS[g], t_len)
    nk, nt = keep // SPAN, t_len // SPAN
    col = lambda j: pl.BlockSpec((SPAN, W_B), lambda b, i, j=j: (b * nt + nt - nk + i, j))
    tab = pl.BlockSpec((SPAN, LANES), lambda b, i: (nt - nk + i, 0))
    return pl.pallas_call(
        _window_rows_kernel,
        grid=(bsz, nk),
        in_specs=[col(7 + g), col(10 + g), tab, tab, tab],
        out_specs=pl.BlockSpec((1, SPAN, 2 * W_B), lambda b, i: (b, i, 0)),
        out_shape=jax.ShapeDtypeStruct((bsz, keep, 2 * W_B), F32),
        compiler_params=_params("parallel", "parallel"),
        name="window_rows",
    )(proj, proj, *tables)


STEP_SLABS = 3 * N_BG + 1


def _step_qkv_kernel(p_ref, cf_ref, sa_ref, sb_ref, o_ref):
    half = ROT_DIM_B // 2
    cf, sa, sb = cf_ref[...], sa_ref[...], sb_ref[...]
    for j in range(2 * N_BG):
        o_ref[:, j * W_B:(j + 1) * W_B] = _rope(p_ref[:, (4 + j) * W_B:(5 + j) * W_B], cf, sa, sb, half)
    o_ref[:, 2 * N_BG * W_B:] = p_ref[:, (4 + 2 * N_BG) * W_B:]


def _step_attn_kernel(x_ref, c1_ref, c2_ref, c3_ref, o_ref, r1_ref, r2_ref, r3_ref, *, t_len):
    scale = DH_B ** -0.5
    c_refs, r_refs = (c1_ref, c2_ref, c3_ref), (r1_ref, r2_ref, r3_ref)
    for g in range(N_BG):
        r_refs[g][0, :, 0] = x_ref[0, :, N_BG + g]
        r_refs[g][0, :, 1] = x_ref[0, :, 2 * N_BG + g]
    for t in range(t_len):
        outs, lses = [], []
        for g in range(N_BG):
            d = B_DILATIONS[g]
            past = min(B_WINDOWS[g], PAST_LEN)
            a_min = past // d + t // d - SPAN
            q = x_ref[0, t, g][None]
            kc, vc = c_refs[g][a_min:, t % d, 0], c_refs[g][a_min:, t % d, 1]
            new = [tt for tt in range(t + 1) if (t - tt) % d == 0 and (t - tt) // d <= SPAN]
            kn = jnp.stack([x_ref[0, tt, N_BG + g] for tt in new])
            vn = jnp.stack([x_ref[0, tt, 2 * N_BG + g] for tt in new])
            s_c = jnp.sum(kc * q, -1, keepdims=True) * scale
            s_n = jnp.sum(kn * q, -1, keepdims=True) * scale
            m = jnp.maximum(jnp.max(s_c, 0, keepdims=True), jnp.max(s_n, 0, keepdims=True))
            e_c, e_n = jnp.exp(s_c - m), jnp.exp(s_n - m)
            den = jnp.sum(e_c, 0, keepdims=True) + jnp.sum(e_n, 0, keepdims=True)
            acc = jnp.sum(e_c * vc, 0, keepdims=True) + jnp.sum(e_n * vn, 0, keepdims=True)
            outs.append(acc / den)
            lses.append(m + jnp.log(den))
        m = jnp.maximum(jnp.maximum(lses[0], lses[1]), lses[2])
        ws = [jnp.exp(l - m) for l in lses]
        merged = (ws[0] * outs[0] + ws[1] * outs[1] + ws[2] * outs[2]) / (ws[0] + ws[1] + ws[2])
        o_ref[0, t] = merged[0] * _silu(x_ref[0, t, 3 * N_BG])


def step_attn(proj, caches, tables, e, bsz, t_len):
    tm = min(bsz, 32) * t_len
    tabs = [jnp.tile(t, (tm // t_len, 1)) for t in tables]
    tab = pl.BlockSpec((tm, LANES), lambda i: (0, 0))
    x = pl.pallas_call(
        _step_qkv_kernel,
        grid=(bsz * t_len // tm,),
        in_specs=[pl.BlockSpec((tm, EVEN_N), lambda i: (i, 0)), tab, tab, tab],
        out_specs=pl.BlockSpec((tm, STEP_SLABS * W_B), lambda i: (i, 0)),
        out_shape=jax.ShapeDtypeStruct((bsz * t_len, STEP_SLABS * W_B), F32),
        compiler_params=_params("parallel"),
        name="step_qkv",
    )(proj, *tabs)
    x = x.reshape(bsz, t_len, STEP_SLABS, H_B, DH_B)
    views, specs = [], []
    for g, c in enumerate(caches):
        d, past = B_DILATIONS[g], c.shape[2]
        assert past % d == 0 and (d <= t_len or d % t_len == 0)
        views.append(c.reshape(c.shape[0], bsz, past // d, d, 2, H_B, DH_B))
        specs.append(pl.BlockSpec((None, None, past // d, min(d, t_len), 2, H_B, DH_B),
                                  lambda b: (e, b, 0, 0, 0, 0, 0)))
    rows = pl.BlockSpec((1, t_len, 2, H_B, DH_B), lambda b: (b, 0, 0, 0, 0))
    return pl.pallas_call(
        functools.partial(_step_attn_kernel, t_len=t_len),
        grid=(bsz,),
        in_specs=[pl.BlockSpec((1, t_len, STEP_SLABS, H_B, DH_B), lambda b: (b, 0, 0, 0, 0))] + specs,
        out_specs=[pl.BlockSpec((1, t_len, H_B, DH_B), lambda b: (b, 0, 0, 0)), rows, rows, rows],
        out_shape=[jax.ShapeDtypeStruct((bsz, t_len, H_B, DH_B), F32)]
        + [jax.ShapeDtypeStruct((bsz, t_len, 2, H_B, DH_B), F32)] * 3,
        compiler_params=_params("parallel"),
        name="step_attn",
    )(x, *views)


def _ssd_kernel(z_ref, x_ref, bc_ref, dt_ref, dtt_ref, cw_ref, cb_ref, dtb_ref, dtbc_ref, nega_ref, negac_ref,
                dskip_ref, ng_ref, expand_ref, conv0_ref, s0_ref,
                y_ref, convn_ref, sfin_ref, ext_ref, st_ref, *, chunk):
    c = pl.program_id(1)
    tail = CONV_W - 1

    @pl.when(c == 0)
    def _():
        ext_ref[0:8] = jnp.concatenate([jnp.zeros((8 - tail, CONV_DIM_C), F32), conv0_ref[0]], axis=0)
        st_ref[...] = s0_ref[0]

    ext_ref[8:8 + chunk, :D_INNER_C] = x_ref[...]
    ext_ref[8:8 + chunk, D_INNER_C:] = bc_ref[...]
    cw = cw_ref[...]
    conv = cb_ref[...] + sum(cw[j:j + 1, :] * ext_ref[8 - tail + j:8 - tail + j + chunk, :] for j in range(CONV_W))
    xbc = _silu(conv)

    @pl.when(c == pl.num_programs(1) - 1)
    def _():
        convn_ref[0] = ext_ref[8 + chunk - tail:8 + chunk, :]

    ext_ref[0:8] = ext_ref[chunk:chunk + 8]
    xs = xbc[:, :D_INNER_C]
    gn = G_C * N_C
    bm = xbc[:, D_INNER_C:D_INNER_C + gn].astype(BF16)
    cm = xbc[:, D_INNER_C + gn:].astype(BF16)

    softplus = lambda a: jnp.maximum(a, 0.0) + jnp.log1p(jnp.exp(-jnp.abs(a)))
    dt = softplus(dt_ref[...] + dtb_ref[...])
    la = dt * nega_ref[...]
    la_t = softplus(dtt_ref[0, 0] + dtbc_ref[...]) * negac_ref[...]
    ii = _iota((chunk, chunk), 0)
    jj = _iota((chunk, chunk), 1)
    causal = ii >= jj
    b = _dot_exact(causal.astype(F32), la)
    b_t = _dot_exact(la_t, (ii <= jj).astype(F32))
    expand = expand_ref[...]
    b_x = _dot_exact(b, expand)
    dt_x = _dot_exact(dt, expand)
    b_last = b_x[chunk - 1:chunk, :]
    xdt = xs * dt_x
    xdt_b = xdt.astype(BF16)
    xdtw = (xdt * jnp.exp(b_last - b_x)).astype(BF16)
    e_b = jnp.exp(b_x)
    low = _iota((chunk, LANES), 1) < P_C
    hg = H_C // G_C
    wg = hg * P_C
    for g in range(G_C):
        cg, bg = cm[:, g * N_C:(g + 1) * N_C], bm[:, g * N_C:(g + 1) * N_C]
        gs = slice(g * wg, (g + 1) * wg)
        y_ref[:, gs] = _dot(cg, st_ref[:, gs].astype(BF16)) * e_b[:, gs]
        scores = _nt(cg, bg)
        for hp in range(hg // 2):
            sl = slice(g * wg + hp * LANES, g * wg + (hp + 1) * LANES)
            res = []
            for h in (g * hg + 2 * hp, g * hg + 2 * hp + 1):
                decay = jnp.exp(jnp.where(causal, b[:, h:h + 1] - b_t[h:h + 1, :], MASK_NEG))
                res.append(_dot((scores * decay).astype(BF16), xdt_b[:, sl]))
            y_ref[:, sl] += jnp.where(low, res[0], res[1])
        st_ref[:, gs] = st_ref[:, gs] * jnp.exp(b_last[:, gs]) + _tn(bg, xdtw[:, gs])

    y = (y_ref[...] + dskip_ref[...] * xs) * _silu(z_ref[...])
    for g in range(G_C):
        gs = slice(g * wg, (g + 1) * wg)
        yg = y[:, gs]
        y_ref[:, gs] = yg * lax.rsqrt(jnp.mean(yg * yg, -1, keepdims=True) + RMS_EPS) * ng_ref[:, gs]

    @pl.when(c == pl.num_programs(1) - 1)
    def _():
        sfin_ref[0] = st_ref[...]


def ssd(proj, dt_t, prm, conv0_all, s0, o, bsz, t_len, chunk):
    nc = t_len // chunk
    blk = lambda w, j: pl.BlockSpec((chunk, w), lambda b, c, j=j: (b * nc + c, j))
    const = lambda a: pl.BlockSpec(a.shape, lambda b, c: (0,) * a.ndim)
    consts = [prm["conv_w"], prm["conv_b"], prm["dt_bias"], prm["dt_bias_col"], prm["neg_a"], prm["neg_a_col"],
              prm["d_skip"], prm["norm_g"], prm["expand"]]
    return pl.pallas_call(
        functools.partial(_ssd_kernel, chunk=chunk),
        grid=(bsz, nc),
        in_specs=[blk(D_INNER_C, 0), blk(D_INNER_C, 1), blk(2 * G_C * N_C, 4), blk(LANES, ODD_MAIN // LANES),
                  pl.BlockSpec((1, 1, H_C, chunk), lambda b, c: (b, c, 0, 0))]
        + [const(a) for a in consts]
        + [pl.BlockSpec((None, 1, CONV_W - 1, CONV_DIM_C), lambda b, c: (o, b, 0, 0)),
           pl.BlockSpec((1, N_C, D_INNER_C), lambda b, c: (b, 0, 0))],
        out_specs=[pl.BlockSpec((chunk, D_INNER_C), lambda b, c: (b * nc + c, 0)),
                   pl.BlockSpec((1, CONV_W - 1, CONV_DIM_C), lambda b, c: (b, 0, 0)),
                   pl.BlockSpec((1, N_C, D_INNER_C), lambda b, c: (b, 0, 0))],
        out_shape=[jax.ShapeDtypeStruct((bsz * t_len, D_INNER_C), F32),
                   jax.ShapeDtypeStruct((bsz, CONV_W - 1, CONV_DIM_C), F32),
                   jax.ShapeDtypeStruct((bsz, N_C, D_INNER_C), F32)],
        scratch_shapes=[pltpu.VMEM((chunk + 8, CONV_DIM_C), F32), pltpu.VMEM((N_C, D_INNER_C), F32)],
        compiler_params=_params("parallel", "arbitrary"),
        name="ssd",
    )(proj, proj, proj, proj, dt_t, *consts, conv0_all, s0)


def _retention_kernel(q_ref, k_ref, v_ref, g_ref, cf_ref, sa_ref, sb_ref, dmat_ref, gq_ref, gw_ref, gs_ref, s0_ref,
                      o_ref, sfin_ref, st_ref, *, chunk):
    c = pl.program_id(1)

    @pl.when(c == 0)
    def _():
        st_ref[...] = s0_ref[0].reshape(H_D * DK_D, DV_D)

    half = DK_D // 2
    cf, sa, sb = cf_ref[...], sa_ref[...], sb_ref[...]
    qr = _rope(q_ref[...], cf, sa, sb, half)
    kr = _rope(k_ref[...], cf, sa, sb, half) * DK_D ** -0.5
    v = v_ref[...].astype(BF16)
    gate = _silu(g_ref[...])
    low = _iota((chunk, LANES), 1) < DK_D
    for hp in range(H_D // 2):
        sl = slice(hp * LANES, (hp + 1) * LANES)
        qp, kp = qr[:, sl], kr[:, sl]
        kp_b = kp.astype(BF16)
        s_pair = st_ref[sl, :]
        s_pair_b = s_pair.astype(BF16)
        upd = s_pair * gs_ref[hp]
        for idx, sel in enumerate((low, ~low)):
            h = 2 * hp + idx
            vh = v[:, h * DV_D:(h + 1) * DV_D]
            qm = jnp.where(sel, qp, 0.0).astype(BF16)
            att = (_nt(qm, kp_b) * dmat_ref[h]).astype(BF16)
            oh = _dot(att, vh) + _dot(qm, s_pair_b) * gq_ref[h]
            mu = jnp.mean(oh, -1, keepdims=True)
            cen = oh - mu
            var = jnp.mean(cen * cen, -1, keepdims=True)
            o_ref[:, h * DV_D:(h + 1) * DV_D] = cen * lax.rsqrt(var + LN_EPS) * gate[:, h * DV_D:(h + 1) * DV_D]
            upd = upd + _tn((jnp.where(sel, kp, 0.0) * gw_ref[h]).astype(BF16), vh)
        st_ref[sl, :] = upd

    @pl.when(c == pl.num_programs(1) - 1)
    def _():
        sfin_ref[0] = st_ref[...].reshape(H_D, DK_D, DV_D)


def retention(proj, tables, consts, s0_all, o, bsz, t_len, chunk):
    nc = t_len // chunk
    blk = lambda w, j: pl.BlockSpec((chunk, w), lambda b, c, j=j: (b * nc + c, j))
    tab = pl.BlockSpec((chunk, LANES), lambda b, c: (c, 0))
    const = lambda a: pl.BlockSpec(a.shape, lambda b, c: (0,) * a.ndim)
    wqk = H_D * DK_D
    return pl.pallas_call(
        functools.partial(_retention_kernel, chunk=chunk),
        grid=(bsz, nc),
        in_specs=[blk(wqk, 2560 // wqk), blk(wqk, 2816 // wqk), blk(W_D, 3072 // W_D), blk(W_D, 3584 // W_D),
                  tab, tab, tab] + [const(a) for a in consts]
        + [pl.BlockSpec((None, 1, H_D, DK_D, DV_D), lambda b, c: (o, b, 0, 0, 0))],
        out_specs=[pl.BlockSpec((chunk, W_D), lambda b, c: (b * nc + c, 0)),
                   pl.BlockSpec((1, H_D, DK_D, DV_D), lambda b, c: (b, 0, 0, 0))],
        out_shape=[jax.ShapeDtypeStruct((bsz * t_len, W_D), F32),
                   jax.ShapeDtypeStruct((bsz, H_D, DK_D, DV_D), F32)],
        scratch_shapes=[pltpu.VMEM((H_D * DK_D, DV_D), F32)],
        compiler_params=_params("parallel", "arbitrary"),
        name="retention",
    )(proj, proj, proj, proj, *tables, *consts, s0_all)


def _retention_consts(chunk):
    lg = np.log1p(-np.exp2(-5.0 - np.arange(H_D, dtype=np.float64)))
    t = np.arange(chunk)
    diff = t[:, None] - t[None, :]
    dmat = np.where(diff >= 0, np.exp(lg[:, None, None] * diff), 0.0)
    gq = np.broadcast_to(np.exp(lg[:, None, None] * (t[None, :, None] + 1)), (H_D, chunk, DV_D))
    gw = np.broadcast_to(np.exp(lg[:, None, None] * (chunk - 1 - t[None, :, None])), (H_D, chunk, LANES))
    gs = np.repeat(np.exp(lg * chunk), DK_D).reshape(H_D // 2, 2 * DK_D, 1)
    gs = np.broadcast_to(gs, (H_D // 2, 2 * DK_D, DV_D))
    return [jnp.asarray(a, F32) for a in (dmat, gq, gw, gs)]


def _mem_attn_kernel(q_ref, kv_ref, o_ref):
    q = q_ref[...].astype(BF16)
    kv = kv_ref[...].astype(BF16)
    for h in range(H_M):
        sl = slice(h * DH_M, (h + 1) * DH_M)
        s = _nt(q[:, sl], kv[:, sl]) * DH_M ** -0.5
        m = jnp.max(s, -1, keepdims=True)
        p = jnp.exp(s - m)
        den = jnp.sum(p, -1, keepdims=True)
        o_ref[:, sl] = (_dot(p.astype(BF16), kv[:, D_MODEL + h * DH_M:D_MODEL + (h + 1) * DH_M]) / den).astype(BF16)


def mem_attn(q, kv, layer, bsz, t_len):
    tq = min(t_len, 512)
    nq = t_len // tq
    return pl.pallas_call(
        _mem_attn_kernel,
        grid=(bsz, nq),
        in_specs=[pl.BlockSpec((tq, D_MODEL), lambda b, i: (b * nq + i, 0)),
                  pl.BlockSpec((None, None, N_MEM, 2 * D_MODEL), lambda b, i: (layer, b, 0, 0))],
        out_specs=pl.BlockSpec((tq, D_MODEL), lambda b, i: (b * nq + i, 0)),
        out_shape=jax.ShapeDtypeStruct((bsz * t_len, D_MODEL), BF16),
        compiler_params=_params("parallel", "arbitrary"),
        name="mem_attn",
    )(q, kv)


def _even_layer(x, xb, e, w_in, w_out, lb, gn, ln_g, ln_b, state_a, bsz, t_len, tables, caches):
    proj = matmul(xb, w_in, 512)
    if caches is None:
        o_a, s_a = hgrn2(proj, lb, gn, state_a, 0, bsz, t_len, GLA_CHUNK, GLA_SUB, GLA_ROWS)
        res = [band_attn(proj, tables, g, bsz, t_len) for g in range(N_BG)]
        o_b = merge_groups([r[0] for r in res], [r[1] for r in res], proj)
        rows = [window_rows(proj, tables, g, bsz, t_len) for g in range(N_BG)]
        rows = [r.reshape(bsz, r.shape[1], 2, H_B, DH_B) for r in rows]
    else:
        o_a, s_a = hgrn2(proj, lb, gn, state_a, e, bsz, t_len, t_len, t_len, t_len)
        o_b, *rows = step_attn(proj, caches, tables, e, bsz, t_len)
        o_b = o_b.reshape(bsz * t_len, W_B).astype(BF16)
    x, xb = outproj_ln([o_a, o_b], [w_out[:W_A], w_out[W_A:]], x, ln_g, ln_b)
    return x, xb, s_a, rows


def _odd_layer(x, xb, o, w_in, w_out, prm, ln_g, ln_b, s_c0, conv0, s_d0, bsz, t_len, chunk, tables, ret_consts):
    proj = matmul(xb, w_in, 384)
    nc = t_len // chunk
    dt_t = proj[:, ODD_MAIN:ODD_MAIN + H_C].reshape(bsz, nc, chunk, H_C).transpose(0, 1, 3, 2)
    y_c, conv_n, s_c = ssd(proj, dt_t, prm, conv0, s_c0, o, bsz, t_len, chunk)
    o_d, s_d = retention(proj, tables, ret_consts, s_d0, o, bsz, t_len, chunk)
    x, xb = outproj_ln([y_c.astype(BF16), o_d.astype(BF16)], [w_out[:D_INNER_C], w_out[D_INNER_C:]], x, ln_g, ln_b)
    return x, xb, s_c, conv_n, s_d


def _memory_layer(x, xb, layer, wq, wo, kv, ln_g, ln_b, bsz, t_len):
    q = matmul(xb, wq, 512)
    att = mem_attn(q, kv, layer, bsz, t_len)
    return outproj_ln([att], [wo], x, ln_g, ln_b)


def _state_to_lanes(s):
    return s.transpose(0, 2, 1, 3).reshape(s.shape[0], N_C, D_INNER_C)


def _state_from_lanes(s):
    return s.reshape(s.shape[0], N_C, H_C, P_C).transpose(0, 2, 1, 3)


def kernel(x_prompt, x_sample, state_a, cache_b1, cache_b2, cache_b3, state_c_ssm, state_c_conv, state_d,
           cache_mem_kv, mem_prompt, w_in_even, a_lb_logits, a_norm_g, w_out_even, w_in_odd, c_conv_w, c_conv_b,
           c_dt_bias, c_a_log, c_d_skip, c_norm_g, w_out_odd, ln1_g, ln1_b, ln2_g, ln2_b, m_wq, m_wkv, m_wo):
    bp, tp = x_prompt.shape[0], x_prompt.shape[1]
    bs, ts = x_sample.shape[0], x_sample.shape[1]
    pos_p = jnp.arange(tp, dtype=jnp.int32)
    pos_s = PAST_LEN + jnp.arange(ts, dtype=jnp.int32)
    tab_b = {"p": _rope_tables(pos_p, ROT_DIM_B, DH_B, ROPE_THETA), "s": _rope_tables(pos_s, ROT_DIM_B, DH_B, ROPE_THETA)}
    tab_d = {"p": _rope_tables(pos_p, DK_D, DK_D, RET_THETA), "s": _rope_tables(pos_s, DK_D, DK_D, RET_THETA)}
    ret_consts = {"p": _retention_consts(RET_CHUNK), "s": _retention_consts(ts)}

    sm = jax.nn.softmax(a_lb_logits.astype(F32), axis=0)
    lb_all = jnp.cumsum(sm, axis=0) - sm[0]
    expand = jnp.asarray(np.pad(np.repeat(np.eye(H_C), P_C, axis=1), ((0, LANES - H_C), (0, 0))), F32)
    pad_h = lambda a: jnp.pad(a.astype(F32), (0, LANES - H_C))[None, :]
    dt0 = D_INNER_C + CONV_DIM_C
    zeros_a = jnp.zeros((1, bp, H_A, DK_A, DV_A), F32)
    zeros_c = jnp.zeros((bp, N_C, D_INNER_C), F32)
    zeros_conv = jnp.zeros((1, bp, CONV_W - 1, CONV_DIM_C), F32)
    zeros_d = jnp.zeros((1, bp, H_D, DK_D, DV_D), F32)
    caches = (cache_b1, cache_b2, cache_b3)
    mem_kv_s = cache_mem_kv.reshape(DEPTH, bs, N_MEM, 2 * D_MODEL)

    xp, xs = x_prompt.reshape(bp * tp, D_MODEL), x_sample.reshape(bs * ts, D_MODEL)
    xpb, xsb = xp.astype(BF16), xs.astype(BF16)
    mem_b = mem_prompt.reshape(bp * N_MEM, D_MODEL).astype(BF16)
    a_p, a_s, c_p, c_s, cv_p, cv_s, d_p, d_s, mem_p = [], [], [], [], [], [], [], [], []
    b_p = [[] for _ in range(N_BG)]
    b_s = [[] for _ in range(N_BG)]
    for l in range(DEPTH):
        if l % 2 == 0:
            e = l // 2
            w_in, w_out = w_in_even[e].astype(BF16), w_out_even[e].astype(BF16)
            lb = lb_all[e][None, :]
            gn = jnp.tile(a_norm_g[e].astype(F32), H_A)[None, :]
            g1, b1 = ln1_g[l][None, :], ln1_b[l][None, :]
            xp, xpb, sap, rp = _even_layer(xp, xpb, e, w_in, w_out, lb, gn, g1, b1, zeros_a, bp, tp, tab_b["p"], None)
            xs, xsb, sas, rs = _even_layer(xs, xsb, e, w_in, w_out, lb, gn, g1, b1, state_a, bs, ts, tab_b["s"], caches)
            a_p.append(sap)
            a_s.append(sas)
            for g in range(N_BG):
                b_p[g].append(rp[g])
                b_s[g].append(rs[g])
        else:
            o = l // 2
            w = w_in_odd[o]
            w_in = jnp.concatenate([w[:, :dt0], w[:, dt0 + H_C:], w[:, dt0:dt0 + H_C],
                                    jnp.zeros((D_MODEL, LANES - H_C), w.dtype)], axis=1).astype(BF16)
            w_out = w_out_odd[o].astype(BF16)
            prm = {
                "conv_w": c_conv_w[o].astype(F32), "conv_b": c_conv_b[o].astype(F32)[None, :],
                "dt_bias": pad_h(c_dt_bias[o]), "dt_bias_col": c_dt_bias[o].astype(F32)[:, None],
                "neg_a": pad_h(-jnp.exp(c_a_log[o].astype(F32))), "neg_a_col": -jnp.exp(c_a_log[o].astype(F32))[:, None],
                "d_skip": jnp.repeat(c_d_skip[o].astype(F32), P_C)[None, :], "norm_g": c_norm_g[o].astype(F32)[None, :],
                "expand": expand,
            }
            g1, b1 = ln1_g[l][None, :], ln1_b[l][None, :]
            xp, xpb, scp, cvp, sdp = _odd_layer(xp, xpb, 0, w_in, w_out, prm, g1, b1, zeros_c, zeros_conv, zeros_d,
                                                bp, tp, SSD_CHUNK, tab_d["p"], ret_consts["p"])
            xs, xsb, scs, cvs, sds = _odd_layer(xs, xsb, o, w_in, w_out, prm, g1, b1, _state_to_lanes(state_c_ssm[o]),
                                                state_c_conv, state_d, bs, ts, ts, tab_d["s"], ret_consts["s"])
            c_p.append(_state_from_lanes(scp))
            c_s.append(_state_from_lanes(scs))
            cv_p.append(cvp)
            cv_s.append(cvs)
            d_p.append(sdp)
            d_s.append(sds)
        kv_p = matmul(mem_b, m_wkv[l].astype(BF16), 512)
        mem_p.append(kv_p.reshape(bp, N_MEM, 2, H_M, DH_M))
        wq, wo = m_wq[l].astype(BF16), m_wo[l].astype(BF16)
        g2, b2 = ln2_g[l][None, :], ln2_b[l][None, :]
        xp, xpb = _memory_layer(xp, xpb, 0, wq, wo, kv_p.reshape(1, bp, N_MEM, 2 * D_MODEL), g2, b2, bp, tp)
        xs, xsb = _memory_layer(xs, xsb, l, wq, wo, mem_kv_s, g2, b2, bs, ts)
    return (xp.reshape(bp, tp, D_MODEL), xs.reshape(bs, ts, D_MODEL), jnp.stack(a_p), jnp.stack(a_s),
            jnp.stack(b_p[0]), jnp.stack(b_p[1]), jnp.stack(b_p[2]),
            jnp.stack(b_s[0]), jnp.stack(b_s[1]), jnp.stack(b_s[2]),
            jnp.stack(c_p), jnp.stack(c_s), jnp.stack(cv_p), jnp.stack(cv_s),
            jnp.stack(d_p), jnp.stack(d_s), jnp.stack(mem_p))
```

```python
import functools
import math

import numpy as np
import jax
import jax.numpy as jnp
from jax import lax
from jax.experimental import pallas as pl
from jax.experimental.pallas import tpu as pltpu

F32 = jnp.float32
BF16 = jnp.bfloat16
HIGHEST = lax.Precision.HIGHEST

D_MODEL = 1024
DEPTH = 4
PAST_LEN = 2048
H_A, DK_A, DV_A = 4, 128, 128
W_A = H_A * DV_A
GLA_CHUNK = 64
GLA_SUB = 16
GLA_ROWS = 256
B_WINDOWS = (128, 512, 2048)
B_DILATIONS = (1, 4, 16)
N_BG, H_B, DH_B = 3, 8, 64
W_B = H_B * DH_B
SPAN = 128
ROT_DIM_B = DH_B // 4
ROPE_THETA = 500000.0
H_C, P_C, G_C, N_C, CONV_W = 16, 64, 2, 128, 4
D_INNER_C = H_C * P_C
CONV_DIM_C = D_INNER_C + 2 * G_C * N_C
SSD_CHUNK = 128
H_D, DK_D, DV_D = 4, 64, 128
W_D = H_D * DV_D
RET_CHUNK = 128
RET_THETA = 10000.0
H_M = 4
DH_M = D_MODEL // H_M
N_MEM = 256

ALPHA = (2.0 * DEPTH) ** 0.25
LN_EPS = 1e-5
RMS_EPS = 1e-6
MASK_NEG = -1e30

LANES = 128
EVEN_N = 7168
ODD_MAIN = 4096
ODD_N = ODD_MAIN + LANES
VMEM_LIMIT = 56 * 1024 * 1024


def _params(*sem):
    return pltpu.CompilerParams(dimension_semantics=sem, vmem_limit_bytes=VMEM_LIMIT)


def _nt(a, b):
    return lax.dot_general(a, b, (((1,), (1,)), ((), ())), preferred_element_type=F32)


def _tn(a, b):
    return lax.dot_general(a, b, (((0,), (0,)), ((), ())), preferred_element_type=F32)


def _dot(a, b):
    return jnp.dot(a, b, preferred_element_type=F32)


def _dot_exact(a, b):
    return jnp.dot(a, b, preferred_element_type=F32, precision=HIGHEST)


def _silu(x):
    return x * jax.nn.sigmoid(x)


def _iota(shape, dim):
    return lax.broadcasted_iota(jnp.int32, shape, dim)


def _rope(x, cf, sa, sb, shift):
    outs = []
    for j in range(x.shape[1] // LANES):
        xj = x[:, j * LANES:(j + 1) * LANES]
        outs.append(xj * cf + pltpu.roll(xj, shift, 1) * sa + pltpu.roll(xj, LANES - shift, 1) * sb)
    return outs[0] if len(outs) == 1 else jnp.concatenate(outs, axis=1)


def _rope_tables(pos, rot_dim, head_dim, theta):
    half = rot_dim // 2
    inv_freq = theta ** (-jnp.arange(half, dtype=F32) / half)
    ang = pos.astype(F32)[:, None] * inv_freq
    cos, sin = jnp.cos(ang), jnp.sin(ang)
    lane = np.arange(LANES) % head_dim
    idx = lane % half
    cf = jnp.where(lane < rot_dim, cos[:, idx], 1.0)
    sa = jnp.where((lane >= half) & (lane < rot_dim), sin[:, idx], 0.0)
    sb = jnp.where(lane < half, -sin[:, idx], 0.0)
    return cf.astype(F32), sa.astype(F32), sb.astype(F32)


def _mm_kernel(x_ref, w_ref, o_ref):
    o_ref[...] = _dot(x_ref[...], w_ref[...])


def matmul(x, w, tn):
    m, k = x.shape
    n = w.shape[1]
    tm = min(m, 1024)
    return pl.pallas_call(
        _mm_kernel,
        grid=(m // tm, n // tn),
        in_specs=[pl.BlockSpec((tm, k), lambda i, j: (i, 0)), pl.BlockSpec((k, tn), lambda i, j: (0, j))],
        out_specs=pl.BlockSpec((tm, tn), lambda i, j: (i, j)),
        out_shape=jax.ShapeDtypeStruct((m, n), F32),
        compiler_params=_params("parallel", "parallel"),
        name="matmul",
    )(x, w)


def _outproj_ln_kernel(*refs, n_parts):
    parts, ws = refs[:n_parts], refs[n_parts:2 * n_parts]
    x_ref, g_ref, b_ref, o_ref, ob_ref = refs[2 * n_parts:]
    acc = ALPHA * x_ref[...]
    for p_ref, w_ref in zip(parts, ws):
        acc = acc + _dot(p_ref[...], w_ref[...])
    mu = jnp.mean(acc, -1, keepdims=True)
    cen = acc - mu
    var = jnp.mean(cen * cen, -1, keepdims=True)
    y = cen * lax.rsqrt(var + LN_EPS) * g_ref[...] + b_ref[...]
    o_ref[...] = y
    ob_ref[...] = y.astype(BF16)


def outproj_ln(parts, ws, x, g, b):
    m = x.shape[0]
    tm = min(m, 512)
    row = lambda w: pl.BlockSpec((tm, w), lambda i: (i, 0))
    const = lambda a: pl.BlockSpec(a.shape, lambda i: (0, 0))
    return pl.pallas_call(
        functools.partial(_outproj_ln_kernel, n_parts=len(parts)),
        grid=(m // tm,),
        in_specs=[row(p.shape[1]) for p in parts] + [const(w) for w in ws] + [row(D_MODEL), const(g), const(b)],
        out_specs=[row(D_MODEL), row(D_MODEL)],
        out_shape=[jax.ShapeDtypeStruct((m, D_MODEL), F32), jax.ShapeDtypeStruct((m, D_MODEL), BF16)],
        compiler_params=_params("parallel"),
        name="outproj_ln",
    )(*parts, *ws, x, g, b)


def _hgrn2_kernel(q_ref, f_ref, v_ref, g_ref, lb_ref, gn_ref, s0_ref, o_ref, sfin_ref, st_ref, *, chunk, sub):
    c = pl.program_id(1)

    @pl.when(c == 0)
    def _():
        for h in range(H_A):
            st_ref[h] = s0_ref[0, h].T

    lb = lb_ref[...]
    gn = gn_ref[...]
    starts = range(0, q_ref.shape[0], chunk)
    nsub = chunk // sub
    tril = (_iota((chunk, chunk), 0) >= _iota((chunk, chunk), 1)).astype(F32)
    head = lambda a, h: a[:, h * DK_A:(h + 1) * DK_A]
    lane_head = _iota((H_A * sub, W_A), 1) // DK_A == _iota((H_A * sub, W_A), 0) // sub

    fx = f_ref[...]
    log_f = jnp.log(lb + (1.0 - lb) * jax.nn.sigmoid(fx))
    k = (1.0 - lb) * jax.nn.sigmoid(-fx)
    q = _silu(q_ref[...])
    v = v_ref[...].astype(BF16)
    bs = [_dot_exact(tril, log_f[c0:c0 + chunk]) for c0 in starts]

    q_in, g_last, kw, q_bd, kt = [], [], [], [], []
    for c0, b in zip(starts, bs):
        qc, kc = q[c0:c0 + chunk], k[c0:c0 + chunk]
        b_last = b[chunk - 1:chunk, :]
        q_in.append((qc * jnp.exp(b)).astype(BF16))
        g_last.append(jnp.exp(b_last))
        kw.append((kc * jnp.exp(b_last - b)).astype(BF16))
        for i in range(nsub):
            r0, r1 = i * sub, (i + 1) * sub
            ref = b[r0 - 1:r0, :] if i > 0 else jnp.zeros((1, W_A), F32)
            qt = qc[r0:r1] * jnp.exp(b[r0:r1] - ref)
            q_bd.append(jnp.where(lane_head, jnp.concatenate([qt] * H_A, axis=0), 0.0).astype(BF16))
            kt.append((kc[:r1] * jnp.exp(ref - b[:r1])).astype(BF16))
    upd = [[_tn(v[c0:c0 + chunk, h * DV_A:(h + 1) * DV_A], head(kw_c, h)) for h in range(H_A)]
           for c0, kw_c in zip(starts, kw)]
    scores = [_nt(a, b_) for a, b_ in zip(q_bd, kt)]
    atts = []
    for idx, s in enumerate(scores):
        r0 = (idx % nsub) * sub
        causal = _iota(s.shape, 1) <= _iota(s.shape, 0) % sub + r0
        atts.append(jnp.where(causal, s, 0.0).astype(BF16))
    intra = []
    for idx, att in enumerate(atts):
        c0 = starts[idx // nsub]
        pv = _dot(att, v[c0:c0 + att.shape[1]])
        intra.append([pv[h * sub:(h + 1) * sub, h * DV_A:(h + 1) * DV_A] for h in range(H_A)])

    gate = _silu(g_ref[...])
    state = [st_ref[h] for h in range(H_A)]
    for ci, c0 in enumerate(starts):
        for h in range(H_A):
            oh = _nt(head(q_in[ci], h), state[h].astype(BF16))
            oh = oh + jnp.concatenate([intra[ci * nsub + i][h] for i in range(nsub)], axis=0)
            normed = oh * lax.rsqrt(jnp.mean(oh * oh, -1, keepdims=True) + RMS_EPS) * head(gn, h)
            o_ref[c0:c0 + chunk, h * DV_A:(h + 1) * DV_A] = (normed * head(gate[c0:c0 + chunk], h)).astype(BF16)
            state[h] = state[h] * head(g_last[ci], h) + upd[ci][h]
    for h in range(H_A):
        st_ref[h] = state[h]

    @pl.when(c == pl.num_programs(1) - 1)
    def _():
        for h in range(H_A):
            sfin_ref[0, h] = st_ref[h].T


def hgrn2(proj, lb, gn, s0_all, e, bsz, t_len, chunk, sub, rows):
    nc = t_len // rows
    col = lambda j: pl.BlockSpec((rows, W_A), lambda b, c, j=j: (b * nc + c, j))
    const = pl.BlockSpec((1, W_A), lambda b, c: (0, 0))
    state = pl.BlockSpec((None, 1, H_A, DK_A, DV_A), lambda b, c: (e, b, 0, 0, 0))
    return pl.pallas_call(
        functools.partial(_hgrn2_kernel, chunk=chunk, sub=sub),
        grid=(bsz, nc),
        in_specs=[col(0), col(1), col(2), col(3), const, const, state],
        out_specs=[pl.BlockSpec((rows, W_A), lambda b, c: (b * nc + c, 0)),
                   pl.BlockSpec((1, H_A, DK_A, DV_A), lambda b, c: (b, 0, 0, 0))],
        out_shape=[jax.ShapeDtypeStruct((bsz * t_len, W_A), BF16),
                   jax.ShapeDtypeStruct((bsz, H_A, DK_A, DV_A), F32)],
        scratch_shapes=[pltpu.VMEM((H_A, DV_A, DK_A), F32)],
        compiler_params=_params("parallel", "arbitrary"),
        name="hgrn2",
    )(proj, proj, proj, proj, lb, gn, s0_all)


BAND_TILE = SPAN * max(B_DILATIONS)


BAND_UNROLL = 4


def _band_attn_kernel(q_ref, k_ref, v_ref, cf_ref, sa_ref, sb_ref, o_ref, l_ref, qs_ref, ks_ref, vs_ref, *, d):
    n = pl.program_id(2)
    half = ROT_DIM_B // 2
    reach = SPAN * d
    nblk = BAND_TILE // reach

    @pl.when(n == 0)
    def _():
        ks_ref[:BAND_TILE] = jnp.zeros((BAND_TILE, LANES), F32)
        vs_ref[:BAND_TILE] = jnp.zeros((BAND_TILE, LANES), F32)

    cf, sa, sb = cf_ref[...], sa_ref[...], sb_ref[...]
    qs_ref[...] = _rope(q_ref[...], cf, sa, sb, half)
    ks_ref[BAND_TILE:] = _rope(k_ref[...], cf, sa, sb, half)
    vs_ref[BAND_TILE:] = v_ref[...]

    i = _iota((SPAN, 2 * SPAN), 0)
    j = _iota((SPAN, 2 * SPAN), 1)
    dist = i + SPAN - j
    band = (dist >= 0) & (dist <= SPAN)
    low = _iota((SPAN, LANES), 1) < DH_B
    strided = lambda start: pl.ds(start, SPAN, stride=d) if d > 1 else pl.ds(start, SPAN)
    for grp in range(0, d * nblk, BAND_UNROLL):
        blocks = [(idx // nblk, idx % nblk) for idx in range(grp, grp + BAND_UNROLL)]
        starts = [jb * reach + r for r, jb in blocks]
        qm, kk, vv = [], [], []
        for st in starts:
            q = qs_ref[strided(st), :]
            qm.append([jnp.where(sel, q, 0.0).astype(BF16) for sel in (low, ~low)])
            cur, prev = strided(BAND_TILE + st), strided(BAND_TILE + st - reach)
            kk.append(jnp.concatenate([ks_ref[prev, :], ks_ref[cur, :]], axis=0).astype(BF16))
            vv.append(jnp.concatenate([vs_ref[prev, :], vs_ref[cur, :]], axis=0).astype(BF16))
        scores = [[_nt(qh, kk_u) * DH_B ** -0.5 for qh in qm_u] for qm_u, kk_u in zip(qm, kk)]
        probs, dens, lses = [], [], []
        for (r, jb), s_u in zip(blocks, scores):
            valid = band & ((j >= SPAN) | (n > 0) | (jb > 0))
            p_u, d_u, l_u = [], [], []
            for s in s_u:
                s = jnp.where(valid, s, MASK_NEG)
                m = jnp.max(s, -1, keepdims=True)
                p = jnp.exp(s - m)
                den = jnp.sum(p, -1, keepdims=True)
                p_u.append(p.astype(BF16))
                d_u.append(den)
                l_u.append(m + jnp.log(den))
            probs.append(p_u)
            dens.append(d_u)
            lses.append(l_u)
        pvs = [[_dot(p, vv_u) for p in p_u] for p_u, vv_u in zip(probs, vv)]
        for st, pv_u, d_u, l_u in zip(starts, pvs, dens, lses):
            o_ref[strided(st), :] = jnp.where(low, pv_u[0] / d_u[0], pv_u[1] / d_u[1])
            l_ref[strided(st), :] = jnp.where(low, l_u[0], l_u[1])

    ks_ref[BAND_TILE - reach:BAND_TILE] = ks_ref[2 * BAND_TILE - reach:]
    vs_ref[BAND_TILE - reach:BAND_TILE] = vs_ref[2 * BAND_TILE - reach:]


def band_attn(proj, tables, g, bsz, t_len):
    d = B_DILATIONS[g]
    nt = t_len // BAND_TILE
    npair = W_B // LANES
    col = lambda j: pl.BlockSpec((BAND_TILE, LANES), lambda b, hp, n, j=j: (b * nt + n, j * npair + hp))
    tab = pl.BlockSpec((BAND_TILE, LANES), lambda b, hp, n: (n, 0))
    out = pl.BlockSpec((BAND_TILE, LANES), lambda b, hp, n: (b * nt + n, hp))
    return pl.pallas_call(
        functools.partial(_band_attn_kernel, d=d),
        grid=(bsz, npair, nt),
        in_specs=[col(4 + g), col(7 + g), col(10 + g), tab, tab, tab],
        out_specs=[out, out],
        out_shape=[jax.ShapeDtypeStruct((bsz * t_len, W_B), F32)] * 2,
        scratch_shapes=[pltpu.VMEM((BAND_TILE, LANES), F32), pltpu.VMEM((2 * BAND_TILE, LANES), F32),
                        pltpu.VMEM((2 * BAND_TILE, LANES), F32)],
        compiler_params=_params("parallel", "parallel", "arbitrary"),
        name=f"band_attn_d{d}",
    )(proj, proj, proj, *tables)


def _merge_kernel(o1, o2, o3, l1, l2, l3, g_ref, o_ref):
    la, lb_, lc = l1[...], l2[...], l3[...]
    m = jnp.maximum(jnp.maximum(la, lb_), lc)
    wa, wb, wc = jnp.exp(la - m), jnp.exp(lb_ - m), jnp.exp(lc - m)
    merged = (wa * o1[...] + wb * o2[...] + wc * o3[...]) / (wa + wb + wc)
    o_ref[...] = (merged * _silu(g_ref[...])).astype(BF16)


def merge_groups(outs, lses, proj):
    m = proj.shape[0]
    tm = 512
    row = pl.BlockSpec((tm, W_B), lambda i: (i, 0))
    return pl.pallas_call(
        _merge_kernel,
        grid=(m // tm,),
        in_specs=[row] * 6 + [pl.BlockSpec((tm, W_B), lambda i: (i, EVEN_N // W_B - 1))],
        out_specs=row,
        out_shape=jax.ShapeDtypeStruct((m, W_B), BF16),
        compiler_params=_params("parallel"),
        name="merge_groups",
    )(*outs, *lses, proj)


def _window_rows_kernel(k_ref, v_ref, cf_ref, sa_ref, sb_ref, o_ref):
    o_ref[0, :, :W_B] = _rope(k_ref[...], cf_ref[...], sa_ref[...], sb_ref[...], ROT_DIM_B // 2)
    o_ref[0, :, W_B:] = v_ref[...]


def window_rows(proj, tables, g, bsz, t_len):
    keep = min(B_WINDOWS[g], t_len)
    nk, nt = keep // SPAN, t_len // SPAN
    col = lambda j: pl.BlockSpec((SPAN, W_B), lambda b, i, j=j: (b * nt + nt - nk + i, j))
    tab = pl.BlockSpec((SPAN, LANES), lambda b, i: (nt - nk + i, 0))
    return pl.pallas_call(
        _window_rows_kernel,
        grid=(bsz, nk),
        in_specs=[col(7 + g), col(10 + g), tab, tab, tab],
        out_specs=pl.BlockSpec((1, SPAN, 2 * W_B), lambda b, i: (b, i, 0)),
        out_shape=jax.ShapeDtypeStruct((bsz, keep, 2 * W_B), F32),
        compiler_params=_params("parallel", "parallel"),
        name="window_rows",
    )(proj, proj, *tables)


def _step_attn_kernel(p_ref, c1_ref, c2_ref, c3_ref, cf_ref, sa_ref, sb_ref, o_ref, r1_ref, r2_ref, r3_ref, *, t_len):
    half = ROT_DIM_B // 2
    cf, sa, sb = cf_ref[...], sa_ref[...], sb_ref[...]
    nrow = H_B * t_len
    own_head = (_iota((nrow, W_B), 1) // DH_B) == (_iota((nrow, W_B), 0) // t_len)
    pad = jnp.zeros((LANES - t_len, W_B), F32)
    outs, lses = [], []
    for g, (c_ref, r_ref) in enumerate(((c1_ref, r1_ref), (c2_ref, r2_ref), (c3_ref, r3_ref))):
        d = B_DILATIONS[g]
        past = c_ref.shape[2]
        q = _rope(p_ref[:, (4 + g) * W_B:(5 + g) * W_B], cf, sa, sb, half)
        k_new = _rope(p_ref[:, (7 + g) * W_B:(8 + g) * W_B], cf, sa, sb, half)
        v_new = p_ref[:, (10 + g) * W_B:(11 + g) * W_B]
        r_ref[0, :, :W_B] = k_new
        r_ref[0, :, W_B:] = v_new
        qb = jnp.where(own_head, jnp.concatenate([q] * H_B, axis=0), 0.0).astype(BF16)
        s_c = _dot(qb, c_ref[0].astype(BF16)) * DH_B ** -0.5
        s_n = _nt(qb, jnp.concatenate([k_new, pad], axis=0).astype(BF16)) * DH_B ** -0.5
        dist = past + _iota(s_c.shape, 0) % t_len - _iota(s_c.shape, 1)
        s_c = jnp.where((dist % d == 0) & (dist // d <= SPAN), s_c, MASK_NEG)
        col = _iota(s_n.shape, 1)
        dist = _iota(s_n.shape, 0) % t_len - col
        s_n = jnp.where((col < t_len) & (dist >= 0) & (dist % d == 0) & (dist // d <= SPAN), s_n, MASK_NEG)
        m = jnp.maximum(jnp.max(s_c, -1, keepdims=True), jnp.max(s_n, -1, keepdims=True))
        e_c, e_n = jnp.exp(s_c - m), jnp.exp(s_n - m)
        den = jnp.sum(e_c, -1, keepdims=True) + jnp.sum(e_n, -1, keepdims=True)
        pv = _nt(e_c.astype(BF16), c_ref[1].astype(BF16))
        pv = pv + _dot(e_n.astype(BF16), jnp.concatenate([v_new, pad], axis=0).astype(BF16))
        outs.append(pv / den)
        lses.append(m + jnp.log(den))
    m = jnp.maximum(jnp.maximum(lses[0], lses[1]), lses[2])
    ws = [jnp.exp(l - m) for l in lses]
    merged = (ws[0] * outs[0] + ws[1] * outs[1] + ws[2] * outs[2]) / (ws[0] + ws[1] + ws[2])
    merged = jnp.where(own_head, merged, 0.0).reshape(H_B, t_len, W_B).sum(axis=0)
    o_ref[...] = (merged * _silu(p_ref[:, 13 * W_B:14 * W_B])).astype(BF16)


def step_attn(proj, caches, tables, e, bsz, t_len):
    views = [c.transpose(0, 1, 3, 4, 5, 2).reshape(c.shape[0], bsz, 2, W_B, c.shape[2]) for c in caches]
    specs = [pl.BlockSpec((None, None, 2, W_B, v.shape[4]), lambda b: (e, b, 0, 0, 0)) for v in views]
    tab = pl.BlockSpec((t_len, LANES), lambda b: (0, 0))
    rows = pl.BlockSpec((1, t_len, 2 * W_B), lambda b: (b, 0, 0))
    return pl.pallas_call(
        functools.partial(_step_attn_kernel, t_len=t_len),
        grid=(bsz,),
        in_specs=[pl.BlockSpec((t_len, EVEN_N), lambda b: (b, 0))] + specs + [tab, tab, tab],
        out_specs=[pl.BlockSpec((t_len, W_B), lambda b: (b, 0)), rows, rows, rows],
        out_shape=[jax.ShapeDtypeStruct((bsz * t_len, W_B), BF16)]
        + [jax.ShapeDtypeStruct((bsz, t_len, 2 * W_B), F32)] * 3,
        compiler_params=_params("parallel"),
        name="step_attn",
    )(proj, *views, *tables)


def _ssd_kernel(z_ref, x_ref, bc_ref, dt_ref, dtt_ref, cw_ref, cb_ref, dtb_ref, dtbc_ref, nega_ref, negac_ref,
                dskip_ref, ng_ref, expand_ref, conv0_ref, s0_ref,
                y_ref, convn_ref, sfin_ref, ext_ref, st_ref, *, chunk):
    c = pl.program_id(1)
    tail = CONV_W - 1

    @pl.when(c == 0)
    def _():
        ext_ref[0:8] = jnp.concatenate([jnp.zeros((8 - tail, CONV_DIM_C), F32), conv0_ref[0]], axis=0)
        st_ref[...] = s0_ref[0]

    ext_ref[8:8 + chunk, :D_INNER_C] = x_ref[...]
    ext_ref[8:8 + chunk, D_INNER_C:] = bc_ref[...]
    cw = cw_ref[...]
    conv = cb_ref[...] + sum(cw[j:j + 1, :] * ext_ref[8 - tail + j:8 - tail + j + chunk, :] for j in range(CONV_W))
    xbc = _silu(conv)

    @pl.when(c == pl.num_programs(1) - 1)
    def _():
        convn_ref[0] = ext_ref[8 + chunk - tail:8 + chunk, :]

    ext_ref[0:8] = ext_ref[chunk:chunk + 8]
    xs = xbc[:, :D_INNER_C]
    gn = G_C * N_C
    bm = xbc[:, D_INNER_C:D_INNER_C + gn].astype(BF16)
    cm = xbc[:, D_INNER_C + gn:].astype(BF16)

    softplus = lambda a: jnp.maximum(a, 0.0) + jnp.log1p(jnp.exp(-jnp.abs(a)))
    dt = softplus(dt_ref[...] + dtb_ref[...])
    la = dt * nega_ref[...]
    la_t = softplus(dtt_ref[0, 0] + dtbc_ref[...]) * negac_ref[...]
    ii = _iota((chunk, chunk), 0)
    jj = _iota((chunk, chunk), 1)
    causal = ii >= jj
    b = _dot_exact(causal.astype(F32), la)
    b_t = _dot_exact(la_t, (ii <= jj).astype(F32))
    expand = expand_ref[...]
    b_x = _dot_exact(b, expand)
    dt_x = _dot_exact(dt, expand)
    b_last = b_x[chunk - 1:chunk, :]
    xdt = xs * dt_x
    xdt_b = xdt.astype(BF16)
    xdtw = (xdt * jnp.exp(b_last - b_x)).astype(BF16)
    e_b = jnp.exp(b_x)
    low = _iota((chunk, LANES), 1) < P_C
    hg = H_C // G_C
    wg = hg * P_C
    for g in range(G_C):
        cg, bg = cm[:, g * N_C:(g + 1) * N_C], bm[:, g * N_C:(g + 1) * N_C]
        gs = slice(g * wg, (g + 1) * wg)
        y_ref[:, gs] = _dot(cg, st_ref[:, gs].astype(BF16)) * e_b[:, gs]
        scores = _nt(cg, bg)
        for hp in range(hg // 2):
            sl = slice(g * wg + hp * LANES, g * wg + (hp + 1) * LANES)
            res = []
            for h in (g * hg + 2 * hp, g * hg + 2 * hp + 1):
                decay = jnp.exp(jnp.where(causal, b[:, h:h + 1] - b_t[h:h + 1, :], MASK_NEG))
                res.append(_dot((scores * decay).astype(BF16), xdt_b[:, sl]))
            y_ref[:, sl] += jnp.where(low, res[0], res[1])
        st_ref[:, gs] = st_ref[:, gs] * jnp.exp(b_last[:, gs]) + _tn(bg, xdtw[:, gs])

    y = (y_ref[...] + dskip_ref[...] * xs) * _silu(z_ref[...])
    for g in range(G_C):
        gs = slice(g * wg, (g + 1) * wg)
        yg = y[:, gs]
        y_ref[:, gs] = yg * lax.rsqrt(jnp.mean(yg * yg, -1, keepdims=True) + RMS_EPS) * ng_ref[:, gs]

    @pl.when(c == pl.num_programs(1) - 1)
    def _():
        sfin_ref[0] = st_ref[...]


def ssd(proj, dt_t, prm, conv0_all, s0, o, bsz, t_len, chunk):
    nc = t_len // chunk
    blk = lambda w, j: pl.BlockSpec((chunk, w), lambda b, c, j=j: (b * nc + c, j))
    const = lambda a: pl.BlockSpec(a.shape, lambda b, c: (0,) * a.ndim)
    consts = [prm["conv_w"], prm["conv_b"], prm["dt_bias"], prm["dt_bias_col"], prm["neg_a"], prm["neg_a_col"],
              prm["d_skip"], prm["norm_g"], prm["expand"]]
    return pl.pallas_call(
        functools.partial(_ssd_kernel, chunk=chunk),
        grid=(bsz, nc),
        in_specs=[blk(D_INNER_C, 0), blk(D_INNER_C, 1), blk(2 * G_C * N_C, 4), blk(LANES, ODD_MAIN // LANES),
                  pl.BlockSpec((1, 1, H_C, chunk), lambda b, c: (b, c, 0, 0))]
        + [const(a) for a in consts]
        + [pl.BlockSpec((None, 1, CONV_W - 1, CONV_DIM_C), lambda b, c: (o, b, 0, 0)),
           pl.BlockSpec((1, N_C, D_INNER_C), lambda b, c: (b, 0, 0))],
        out_specs=[pl.BlockSpec((chunk, D_INNER_C), lambda b, c: (b * nc + c, 0)),
                   pl.BlockSpec((1, CONV_W - 1, CONV_DIM_C), lambda b, c: (b, 0, 0)),
                   pl.BlockSpec((1, N_C, D_INNER_C), lambda b, c: (b, 0, 0))],
        out_shape=[jax.ShapeDtypeStruct((bsz * t_len, D_INNER_C), F32),
                   jax.ShapeDtypeStruct((bsz, CONV_W - 1, CONV_DIM_C), F32),
                   jax.ShapeDtypeStruct((bsz, N_C, D_INNER_C), F32)],
        scratch_shapes=[pltpu.VMEM((chunk + 8, CONV_DIM_C), F32), pltpu.VMEM((N_C, D_INNER_C), F32)],
        compiler_params=_params("parallel", "arbitrary"),
        name="ssd",
    )(proj, proj, proj, proj, dt_t, *consts, conv0_all, s0)


def _retention_kernel(q_ref, k_ref, v_ref, g_ref, cf_ref, sa_ref, sb_ref, dmat_ref, gq_ref, gw_ref, gs_ref, s0_ref,
                      o_ref, sfin_ref, st_ref, *, chunk):
    c = pl.program_id(1)

    @pl.when(c == 0)
    def _():
        st_ref[...] = s0_ref[0].reshape(H_D * DK_D, DV_D)

    half = DK_D // 2
    cf, sa, sb = cf_ref[...], sa_ref[...], sb_ref[...]
    qr = _rope(q_ref[...], cf, sa, sb, half)
    kr = _rope(k_ref[...], cf, sa, sb, half) * DK_D ** -0.5
    v = v_ref[...].astype(BF16)
    gate = _silu(g_ref[...])
    low = _iota((chunk, LANES), 1) < DK_D
    for hp in range(H_D // 2):
        sl = slice(hp * LANES, (hp + 1) * LANES)
        qp, kp = qr[:, sl], kr[:, sl]
        kp_b = kp.astype(BF16)
        s_pair = st_ref[sl, :]
        s_pair_b = s_pair.astype(BF16)
        upd = s_pair * gs_ref[hp]
        for idx, sel in enumerate((low, ~low)):
            h = 2 * hp + idx
            vh = v[:, h * DV_D:(h + 1) * DV_D]
            qm = jnp.where(sel, qp, 0.0).astype(BF16)
            att = (_nt(qm, kp_b) * dmat_ref[h]).astype(BF16)
            oh = _dot(att, vh) + _dot(qm, s_pair_b) * gq_ref[h]
            mu = jnp.mean(oh, -1, keepdims=True)
            cen = oh - mu
            var = jnp.mean(cen * cen, -1, keepdims=True)
            o_ref[:, h * DV_D:(h + 1) * DV_D] = cen * lax.rsqrt(var + LN_EPS) * gate[:, h * DV_D:(h + 1) * DV_D]
            upd = upd + _tn((jnp.where(sel, kp, 0.0) * gw_ref[h]).astype(BF16), vh)
        st_ref[sl, :] = upd

    @pl.when(c == pl.num_programs(1) - 1)
    def _():
        sfin_ref[0] = st_ref[...].reshape(H_D, DK_D, DV_D)


def retention(proj, tables, consts, s0_all, o, bsz, t_len, chunk):
    nc = t_len // chunk
    blk = lambda w, j: pl.BlockSpec((chunk, w), lambda b, c, j=j: (b * nc + c, j))
    tab = pl.BlockSpec((chunk, LANES), lambda b, c: (c, 0))
    const = lambda a: pl.BlockSpec(a.shape, lambda b, c: (0,) * a.ndim)
    wqk = H_D * DK_D
    return pl.pallas_call(
        functools.partial(_retention_kernel, chunk=chunk),
        grid=(bsz, nc),
        in_specs=[blk(wqk, 2560 // wqk), blk(wqk, 2816 // wqk), blk(W_D, 3072 // W_D), blk(W_D, 3584 // W_D),
                  tab, tab, tab] + [const(a) for a in consts]
        + [pl.BlockSpec((None, 1, H_D, DK_D, DV_D), lambda b, c: (o, b, 0, 0, 0))],
        out_specs=[pl.BlockSpec((chunk, W_D), lambda b, c: (b * nc + c, 0)),
                   pl.BlockSpec((1, H_D, DK_D, DV_D), lambda b, c: (b, 0, 0, 0))],
        out_shape=[jax.ShapeDtypeStruct((bsz * t_len, W_D), F32),
                   jax.ShapeDtypeStruct((bsz, H_D, DK_D, DV_D), F32)],
        scratch_shapes=[pltpu.VMEM((H_D * DK_D, DV_D), F32)],
        compiler_params=_params("parallel", "arbitrary"),
        name="retention",
    )(proj, proj, proj, proj, *tables, *consts, s0_all)


def _retention_consts(chunk):
    lg = np.log1p(-np.exp2(-5.0 - np.arange(H_D, dtype=np.float64)))
    t = np.arange(chunk)
    diff = t[:, None] - t[None, :]
    dmat = np.where(diff >= 0, np.exp(lg[:, None, None] * diff), 0.0)
    gq = np.broadcast_to(np.exp(lg[:, None, None] * (t[None, :, None] + 1)), (H_D, chunk, DV_D))
    gw = np.broadcast_to(np.exp(lg[:, None, None] * (chunk - 1 - t[None, :, None])), (H_D, chunk, LANES))
    gs = np.repeat(np.exp(lg * chunk), DK_D).reshape(H_D // 2, 2 * DK_D, 1)
    gs = np.broadcast_to(gs, (H_D // 2, 2 * DK_D, DV_D))
    return [jnp.asarray(a, F32) for a in (dmat, gq, gw, gs)]


MEM_LANE_TILES = DH_M // LANES
MEM_ROW_PERIOD = 2 * MEM_LANE_TILES * H_M


def mem_kv_rows(kv):
    lead = kv.shape[:-4]
    kv = kv.reshape(*lead, N_MEM, 2, H_M, MEM_LANE_TILES, LANES)
    return jnp.swapaxes(kv, -3, -2).reshape(*lead, N_MEM * MEM_ROW_PERIOD, LANES)


def _mem_attn_kernel(q_ref, kv_ref, o_ref):
    q = q_ref[...].astype(BF16)

    def head_rows(which, h):
        tiles = [kv_ref[pl.ds((which * MEM_LANE_TILES + j) * H_M + h, N_MEM, stride=MEM_ROW_PERIOD), :]
                 for j in range(MEM_LANE_TILES)]
        return jnp.concatenate(tiles, axis=1).astype(BF16)

    heads = [slice(h * DH_M, (h + 1) * DH_M) for h in range(H_M)]
    scores = [_nt(q[:, sl], head_rows(0, h)) * DH_M ** -0.5 for h, sl in enumerate(heads)]
    probs, dens = [], []
    for s in scores:
        p = jnp.exp(s - jnp.max(s, -1, keepdims=True))
        dens.append(jnp.sum(p, -1, keepdims=True))
        probs.append(p.astype(BF16))
    outs = [_dot(p, head_rows(1, h)) for h, p in enumerate(probs)]
    for sl, o, den in zip(heads, outs, dens):
        o_ref[:, sl] = (o / den).astype(BF16)


def mem_attn(q, kv, layer, bsz, t_len):
    tq = min(t_len, 512)
    nq = t_len // tq
    return pl.pallas_call(
        _mem_attn_kernel,
        grid=(bsz, nq),
        in_specs=[pl.BlockSpec((tq, D_MODEL), lambda b, i: (b * nq + i, 0)),
                  pl.BlockSpec((None, None, N_MEM * MEM_ROW_PERIOD, LANES), lambda b, i: (layer, b, 0, 0))],
        out_specs=pl.BlockSpec((tq, D_MODEL), lambda b, i: (b * nq + i, 0)),
        out_shape=jax.ShapeDtypeStruct((bsz * t_len, D_MODEL), BF16),
        compiler_params=_params("parallel", "arbitrary"),
        name="mem_attn",
    )(q, kv)


def _even_layer(x, xb, e, w_in, w_out, lb, gn, ln_g, ln_b, state_a, bsz, t_len, tables, caches):
    proj = matmul(xb, w_in, 512)
    if caches is None:
        o_a, s_a = hgrn2(proj, lb, gn, state_a, 0, bsz, t_len, GLA_CHUNK, GLA_SUB, GLA_ROWS)
        res = [band_attn(proj, tables, g, bsz, t_len) for g in range(N_BG)]
        o_b = merge_groups([r[0] for r in res], [r[1] for r in res], proj)
        rows = [window_rows(proj, tables, g, bsz, t_len) for g in range(N_BG)]
    else:
        o_a, s_a = hgrn2(proj, lb, gn, state_a, e, bsz, t_len, t_len, t_len, t_len)
        o_b, *rows = step_attn(proj, caches, tables, e, bsz, t_len)
    x, xb = outproj_ln([o_a, o_b], [w_out[:W_A], w_out[W_A:]], x, ln_g, ln_b)
    rows = [r.reshape(bsz, r.shape[1], 2, H_B, DH_B) for r in rows]
    return x, xb, s_a, rows


def _odd_layer(x, xb, o, w_in, w_out, prm, ln_g, ln_b, s_c0, conv0, s_d0, bsz, t_len, chunk, tables, ret_consts):
    proj = matmul(xb, w_in, 384)
    nc = t_len // chunk
    dt_t = proj[:, ODD_MAIN:ODD_MAIN + H_C].reshape(bsz, nc, chunk, H_C).transpose(0, 1, 3, 2)
    y_c, conv_n, s_c = ssd(proj, dt_t, prm, conv0, s_c0, o, bsz, t_len, chunk)
    o_d, s_d = retention(proj, tables, ret_consts, s_d0, o, bsz, t_len, chunk)
    x, xb = outproj_ln([y_c.astype(BF16), o_d.astype(BF16)], [w_out[:D_INNER_C], w_out[D_INNER_C:]], x, ln_g, ln_b)
    return x, xb, s_c, conv_n, s_d


def _memory_layer(x, xb, layer, wq, wo, kv, ln_g, ln_b, bsz, t_len):
    q = matmul(xb, wq, 512)
    att = mem_attn(q, kv, layer, bsz, t_len)
    return outproj_ln([att], [wo], x, ln_g, ln_b)


def _state_to_lanes(s):
    return s.transpose(0, 2, 1, 3).reshape(s.shape[0], N_C, D_INNER_C)


def _state_from_lanes(s):
    return s.reshape(s.shape[0], N_C, H_C, P_C).transpose(0, 2, 1, 3)


def kernel(x_prompt, x_sample, state_a, cache_b1, cache_b2, cache_b3, state_c_ssm, state_c_conv, state_d,
           cache_mem_kv, mem_prompt, w_in_even, a_lb_logits, a_norm_g, w_out_even, w_in_odd, c_conv_w, c_conv_b,
           c_dt_bias, c_a_log, c_d_skip, c_norm_g, w_out_odd, ln1_g, ln1_b, ln2_g, ln2_b, m_wq, m_wkv, m_wo):
    bp, tp = x_prompt.shape[0], x_prompt.shape[1]
    bs, ts = x_sample.shape[0], x_sample.shape[1]
    pos_p = jnp.arange(tp, dtype=jnp.int32)
    pos_s = PAST_LEN + jnp.arange(ts, dtype=jnp.int32)
    tab_b = {"p": _rope_tables(pos_p, ROT_DIM_B, DH_B, ROPE_THETA), "s": _rope_tables(pos_s, ROT_DIM_B, DH_B, ROPE_THETA)}
    tab_d = {"p": _rope_tables(pos_p, DK_D, DK_D, RET_THETA), "s": _rope_tables(pos_s, DK_D, DK_D, RET_THETA)}
    ret_consts = {"p": _retention_consts(RET_CHUNK), "s": _retention_consts(ts)}

    sm = jax.nn.softmax(a_lb_logits.astype(F32), axis=0)
    lb_all = jnp.cumsum(sm, axis=0) - sm[0]
    expand = jnp.asarray(np.pad(np.repeat(np.eye(H_C), P_C, axis=1), ((0, LANES - H_C), (0, 0))), F32)
    pad_h = lambda a: jnp.pad(a.astype(F32), (0, LANES - H_C))[None, :]
    dt0 = D_INNER_C + CONV_DIM_C
    zeros_a = jnp.zeros((1, bp, H_A, DK_A, DV_A), F32)
    zeros_c = jnp.zeros((bp, N_C, D_INNER_C), F32)
    zeros_conv = jnp.zeros((1, bp, CONV_W - 1, CONV_DIM_C), F32)
    zeros_d = jnp.zeros((1, bp, H_D, DK_D, DV_D), F32)
    caches = (cache_b1, cache_b2, cache_b3)
    mem_kv_s = mem_kv_rows(cache_mem_kv)

    xp, xs = x_prompt.reshape(bp * tp, D_MODEL), x_sample.reshape(bs * ts, D_MODEL)
    xpb, xsb = xp.astype(BF16), xs.astype(BF16)
    mem_b = mem_prompt.reshape(bp * N_MEM, D_MODEL).astype(BF16)
    a_p, a_s, c_p, c_s, cv_p, cv_s, d_p, d_s, mem_p = [], [], [], [], [], [], [], [], []
    b_p = [[] for _ in range(N_BG)]
    b_s = [[] for _ in range(N_BG)]
    for l in range(DEPTH):
        if l % 2 == 0:
            e = l // 2
            w_in, w_out = w_in_even[e].astype(BF16), w_out_even[e].astype(BF16)
            lb = lb_all[e][None, :]
            gn = jnp.tile(a_norm_g[e].astype(F32), H_A)[None, :]
            g1, b1 = ln1_g[l][None, :], ln1_b[l][None, :]
            xp, xpb, sap, rp = _even_layer(xp, xpb, e, w_in, w_out, lb, gn, g1, b1, zeros_a, bp, tp, tab_b["p"], None)
            xs, xsb, sas, rs = _even_layer(xs, xsb, e, w_in, w_out, lb, gn, g1, b1, state_a, bs, ts, tab_b["s"], caches)
            a_p.append(sap)
            a_s.append(sas)
            for g in range(N_BG):
                b_p[g].append(rp[g])
                b_s[g].append(rs[g])
        else:
            o = l // 2
            w = w_in_odd[o]
            w_in = jnp.concatenate([w[:, :dt0], w[:, dt0 + H_C:], w[:, dt0:dt0 + H_C],
                                    jnp.zeros((D_MODEL, LANES - H_C), w.dtype)], axis=1).astype(BF16)
            w_out = w_out_odd[o].astype(BF16)
            prm = {
                "conv_w": c_conv_w[o].astype(F32), "conv_b": c_conv_b[o].astype(F32)[None, :],
                "dt_bias": pad_h(c_dt_bias[o]), "dt_bias_col": c_dt_bias[o].astype(F32)[:, None],
                "neg_a": pad_h(-jnp.exp(c_a_log[o].astype(F32))), "neg_a_col": -jnp.exp(c_a_log[o].astype(F32))[:, None],
                "d_skip": jnp.repeat(c_d_skip[o].astype(F32), P_C)[None, :], "norm_g": c_norm_g[o].astype(F32)[None, :],
                "expand": expand,
            }
            g1, b1 = ln1_g[l][None, :], ln1_b[l][None, :]
            xp, xpb, scp, cvp, sdp = _odd_layer(xp, xpb, 0, w_in, w_out, prm, g1, b1, zeros_c, zeros_conv, zeros_d,
                                                bp, tp, SSD_CHUNK, tab_d["p"], ret_consts["p"])
            xs, xsb, scs, cvs, sds = _odd_layer(xs, xsb, o, w_in, w_out, prm, g1, b1, _state_to_lanes(state_c_ssm[o]),
                                                state_c_conv, state_d, bs, ts, ts, tab_d["s"], ret_consts["s"])
            c_p.append(_state_from_lanes(scp))
            c_s.append(_state_from_lanes(scs))
            cv_p.append(cvp)
            cv_s.append(cvs)
            d_p.append(sdp)
            d_s.append(sds)
        kv_p = matmul(mem_b, m_wkv[l].astype(BF16), 512)
        kv_p = kv_p.reshape(bp, N_MEM, 2, H_M, DH_M)
        mem_p.append(kv_p)
        wq, wo = m_wq[l].astype(BF16), m_wo[l].astype(BF16)
        g2, b2 = ln2_g[l][None, :], ln2_b[l][None, :]
        xp, xpb = _memory_layer(xp, xpb, 0, wq, wo, mem_kv_rows(kv_p)[None], g2, b2, bp, tp)
        xs, xsb = _memory_layer(xs, xsb, l, wq, wo, mem_kv_s, g2, b2, bs, ts)
    return (xp.reshape(bp, tp, D_MODEL), xs.reshape(bs, ts, D_MODEL), jnp.stack(a_p), jnp.stack(a_s),
            jnp.stack(b_p[0]), jnp.stack(b_p[1]), jnp.stack(b_p[2]),
            jnp.stack(b_s[0]), jnp.stack(b_s[1]), jnp.stack(b_s[2]),
            jnp.stack(c_p), jnp.stack(c_s), jnp.stack(cv_p), jnp.stack(cv_s),
            jnp.stack(d_p), jnp.stack(d_s), jnp.stack(mem_p))
```

```python
import functools
import math

import numpy as np
import jax
import jax.numpy as jnp
from jax import lax
from jax.experimental import pallas as pl
from jax.experimental.pallas import tpu as pltpu

F32 = jnp.float32
BF16 = jnp.bfloat16
HIGHEST = lax.Precision.HIGHEST

D_MODEL = 1024
DEPTH = 4
PAST_LEN = 2048
H_A, DK_A, DV_A = 4, 128, 128
W_A = H_A * DV_A
GLA_CHUNK = 64
GLA_SUB = 16
GLA_ROWS = 256
STEP_SEQS = 4
B_WINDOWS = (128, 512, 2048)
B_DILATIONS = (1, 4, 16)
N_BG, H_B, DH_B = 3, 8, 64
W_B = H_B * DH_B
SPAN = 128
ROT_DIM_B = DH_B // 4
ROPE_THETA = 500000.0
H_C, P_C, G_C, N_C, CONV_W = 16, 64, 2, 128, 4
D_INNER_C = H_C * P_C
CONV_DIM_C = D_INNER_C + 2 * G_C * N_C
SSD_CHUNK = 128
H_D, DK_D, DV_D = 4, 64, 128
W_D = H_D * DV_D
RET_CHUNK = 128
RET_THETA = 10000.0
H_M = 4
DH_M = D_MODEL // H_M
N_MEM = 256

ALPHA = (2.0 * DEPTH) ** 0.25
LN_EPS = 1e-5
RMS_EPS = 1e-6
MASK_NEG = -1e30

LANES = 128
EVEN_N = 7168
ODD_MAIN = 4096
ODD_N = ODD_MAIN + LANES
VMEM_LIMIT = 56 * 1024 * 1024


def _params(*sem):
    return pltpu.CompilerParams(dimension_semantics=sem, vmem_limit_bytes=VMEM_LIMIT)


def _nt(a, b):
    return lax.dot_general(a, b, (((1,), (1,)), ((), ())), preferred_element_type=F32)


def _tn(a, b):
    return lax.dot_general(a, b, (((0,), (0,)), ((), ())), preferred_element_type=F32)


def _dot(a, b):
    return jnp.dot(a, b, preferred_element_type=F32)


def _dot_exact(a, b):
    return jnp.dot(a, b, preferred_element_type=F32, precision=HIGHEST)


def _dot_select(sel, x, sel_is_lhs):
    hi = x.astype(BF16)
    rest = x - hi.astype(F32)
    mid = rest.astype(BF16)
    lo = (rest - mid.astype(F32)).astype(BF16)
    mm = (lambda t: _dot(sel, t)) if sel_is_lhs else (lambda t: _dot(t, sel))
    return mm(hi) + mm(mid) + mm(lo)


def _silu(x):
    return x * jax.nn.sigmoid(x)


def _iota(shape, dim):
    return lax.broadcasted_iota(jnp.int32, shape, dim)


def _rope(x, cf, sa, sb, shift):
    outs = []
    for j in range(x.shape[1] // LANES):
        xj = x[:, j * LANES:(j + 1) * LANES]
        outs.append(xj * cf + pltpu.roll(xj, shift, 1) * sa + pltpu.roll(xj, LANES - shift, 1) * sb)
    return outs[0] if len(outs) == 1 else jnp.concatenate(outs, axis=1)


def _rope_tables(pos, rot_dim, head_dim, theta):
    half = rot_dim // 2
    inv_freq = theta ** (-jnp.arange(half, dtype=F32) / half)
    ang = pos.astype(F32)[:, None] * inv_freq
    cos, sin = jnp.cos(ang), jnp.sin(ang)
    lane = np.arange(LANES) % head_dim
    idx = lane % half
    cf = jnp.where(lane < rot_dim, cos[:, idx], 1.0)
    sa = jnp.where((lane >= half) & (lane < rot_dim), sin[:, idx], 0.0)
    sb = jnp.where(lane < half, -sin[:, idx], 0.0)
    return cf.astype(F32), sa.astype(F32), sb.astype(F32)


def _mm_kernel(x_ref, w_ref, o_ref):
    o_ref[...] = _dot(x_ref[...], w_ref[...])


def matmul(x, w, tn):
    m, k = x.shape
    n = w.shape[1]
    tm = min(m, 2048)
    return pl.pallas_call(
        _mm_kernel,
        grid=(m // tm, n // tn),
        in_specs=[pl.BlockSpec((tm, k), lambda i, j: (i, 0)), pl.BlockSpec((k, tn), lambda i, j: (0, j))],
        out_specs=pl.BlockSpec((tm, tn), lambda i, j: (i, j)),
        out_shape=jax.ShapeDtypeStruct((m, n), F32),
        compiler_params=_params("parallel", "parallel"),
        name="matmul",
    )(x, w)


def _outproj_ln_kernel(*refs, n_parts):
    parts, ws = refs[:n_parts], refs[n_parts:2 * n_parts]
    x_ref, g_ref, b_ref, o_ref, ob_ref = refs[2 * n_parts:]
    acc = ALPHA * x_ref[...]
    for p_ref, w_ref in zip(parts, ws):
        acc = acc + _dot(p_ref[...], w_ref[...])
    mu = jnp.mean(acc, -1, keepdims=True)
    cen = acc - mu
    var = jnp.mean(cen * cen, -1, keepdims=True)
    y = cen * lax.rsqrt(var + LN_EPS) * g_ref[...] + b_ref[...]
    o_ref[...] = y
    ob_ref[...] = y.astype(BF16)


def outproj_ln(parts, ws, x, g, b):
    m = x.shape[0]
    tm = min(m, 512)
    row = lambda w: pl.BlockSpec((tm, w), lambda i: (i, 0))
    const = lambda a: pl.BlockSpec(a.shape, lambda i: (0, 0))
    return pl.pallas_call(
        functools.partial(_outproj_ln_kernel, n_parts=len(parts)),
        grid=(m // tm,),
        in_specs=[row(p.shape[1]) for p in parts] + [const(w) for w in ws] + [row(D_MODEL), const(g), const(b)],
        out_specs=[row(D_MODEL), row(D_MODEL)],
        out_shape=[jax.ShapeDtypeStruct((m, D_MODEL), F32), jax.ShapeDtypeStruct((m, D_MODEL), BF16)],
        compiler_params=_params("parallel"),
        name="outproj_ln",
    )(*parts, *ws, x, g, b)


def _hgrn2_kernel(q_ref, f_ref, v_ref, g_ref, lb_ref, gn_ref, s0_ref, o_ref, sfin_ref, st_ref, *, chunk, sub):
    c = pl.program_id(1)
    nseq = s0_ref.shape[0]
    seq_rows = q_ref.shape[0] // nseq

    @pl.when(c == 0)
    def _():
        for i in range(nseq):
            for h in range(H_A):
                st_ref[i * H_A + h] = s0_ref[i, h].T

    lb = lb_ref[...]
    gn = gn_ref[...]
    starts = range(0, q_ref.shape[0], chunk)
    nsub = chunk // sub
    tril = _iota((chunk, chunk), 0) >= _iota((chunk, chunk), 1)
    if chunk % 16 == 0:
        cumsum = lambda a: _dot_select(tril.astype(BF16), a, True)
    else:
        cumsum = lambda a: _dot_exact(tril.astype(F32), a)
    head = lambda a, h: a[:, h * DK_A:(h + 1) * DK_A]
    lane_head = _iota((H_A * sub, W_A), 1) // DK_A == _iota((H_A * sub, W_A), 0) // sub

    fx = f_ref[...]
    log_f = jnp.log(lb + (1.0 - lb) * jax.nn.sigmoid(fx))
    k = (1.0 - lb) * jax.nn.sigmoid(-fx)
    q = _silu(q_ref[...])
    v = v_ref[...].astype(BF16)
    bs = [cumsum(log_f[c0:c0 + chunk]) for c0 in starts]

    q_in, g_last, kw, q_bd, kt = [], [], [], [], []
    for c0, b in zip(starts, bs):
        qc, kc = q[c0:c0 + chunk], k[c0:c0 + chunk]
        b_last = b[chunk - 1:chunk, :]
        q_in.append((qc * jnp.exp(b)).astype(BF16))
        g_last.append(jnp.exp(b_last))
        kw.append((kc * jnp.exp(b_last - b)).astype(BF16))
        for i in range(nsub):
            r0, r1 = i * sub, (i + 1) * sub
            ref = b[r0 - 1:r0, :] if i > 0 else jnp.zeros((1, W_A), F32)
            qt = qc[r0:r1] * jnp.exp(b[r0:r1] - ref)
            q_bd.append(jnp.where(lane_head, jnp.concatenate([qt] * H_A, axis=0), 0.0).astype(BF16))
            kt.append((kc[:r1] * jnp.exp(ref - b[:r1])).astype(BF16))
    upd = [[_tn(v[c0:c0 + chunk, h * DV_A:(h + 1) * DV_A], head(kw_c, h)) for h in range(H_A)]
           for c0, kw_c in zip(starts, kw)]
    scores = [_nt(a, b_) for a, b_ in zip(q_bd, kt)]
    atts = []
    for idx, s in enumerate(scores):
        r0 = (idx % nsub) * sub
        causal = _iota(s.shape, 1) <= _iota(s.shape, 0) % sub + r0
        atts.append(jnp.where(causal, s, 0.0).astype(BF16))
    intra = []
    for idx, att in enumerate(atts):
        c0 = starts[idx // nsub]
        pv = _dot(att, v[c0:c0 + att.shape[1]])
        intra.append([pv[h * sub:(h + 1) * sub, h * DV_A:(h + 1) * DV_A] for h in range(H_A)])

    gate = _silu(g_ref[...])
    state = [st_ref[j] for j in range(nseq * H_A)]
    outs = [[] for _ in range(H_A)]
    for ci, c0 in enumerate(starts):
        for h in range(H_A):
            j = (c0 // seq_rows) * H_A + h
            oh = _nt(head(q_in[ci], h), state[j].astype(BF16))
            oh = oh + jnp.concatenate([intra[ci * nsub + i][h] for i in range(nsub)], axis=0)
            outs[h].append(oh * lax.rsqrt(jnp.mean(oh * oh, -1, keepdims=True) + RMS_EPS) * head(gn, h))
            state[j] = state[j] * head(g_last[ci], h) + upd[ci][h]
    for h in range(H_A):
        o_ref[:, h * DV_A:(h + 1) * DV_A] = (jnp.concatenate(outs[h], axis=0) * head(gate, h)).astype(BF16)
    for j in range(nseq * H_A):
        st_ref[j] = state[j]

    @pl.when(c == pl.num_programs(1) - 1)
    def _():
        for i in range(nseq):
            for h in range(H_A):
                sfin_ref[i, h] = st_ref[i * H_A + h].T


def hgrn2(proj, lb, gn, s0_all, e, bsz, t_len, chunk, sub, rows, nseq=1):
    nc = t_len // rows
    assert nseq == 1 or nc == 1
    col = lambda j: pl.BlockSpec((nseq * rows, W_A), lambda b, c, j=j: (b * nc + c, j))
    const = pl.BlockSpec((1, W_A), lambda b, c: (0, 0))
    state = pl.BlockSpec((None, nseq, H_A, DK_A, DV_A), lambda b, c: (e, b, 0, 0, 0))
    return pl.pallas_call(
        functools.partial(_hgrn2_kernel, chunk=chunk, sub=sub),
        grid=(bsz // nseq, nc),
        in_specs=[col(0), col(1), col(2), col(3), const, const, state],
        out_specs=[pl.BlockSpec((nseq * rows, W_A), lambda b, c: (b * nc + c, 0)),
                   pl.BlockSpec((nseq, H_A, DK_A, DV_A), lambda b, c: (b, 0, 0, 0))],
        out_shape=[jax.ShapeDtypeStruct((bsz * t_len, W_A), BF16),
                   jax.ShapeDtypeStruct((bsz, H_A, DK_A, DV_A), F32)],
        scratch_shapes=[pltpu.VMEM((nseq * H_A, DV_A, DK_A), F32)],
        compiler_params=_params("parallel", "arbitrary"),
        name="hgrn2",
    )(proj, proj, proj, proj, lb, gn, s0_all)


BAND_TILE = SPAN * max(B_DILATIONS)


BAND_UNROLL = 4


def _band_attn_kernel(q_ref, k_ref, v_ref, cf_ref, sa_ref, sb_ref, o_ref, l_ref, qs_ref, ks_ref, vs_ref, *, d):
    n = pl.program_id(2)
    half = ROT_DIM_B // 2
    reach = SPAN * d
    nblk = BAND_TILE // reach

    @pl.when(n == 0)
    def _():
        ks_ref[:BAND_TILE] = jnp.zeros((BAND_TILE, LANES), F32)
        vs_ref[:BAND_TILE] = jnp.zeros((BAND_TILE, LANES), F32)

    cf, sa, sb = cf_ref[...], sa_ref[...], sb_ref[...]
    qs_ref[...] = _rope(q_ref[...], cf, sa, sb, half)
    ks_ref[BAND_TILE:] = _rope(k_ref[...], cf, sa, sb, half)
    vs_ref[BAND_TILE:] = v_ref[...]

    i = _iota((SPAN, 2 * SPAN), 0)
    j = _iota((SPAN, 2 * SPAN), 1)
    dist = i + SPAN - j
    band = (dist >= 0) & (dist <= SPAN)
    low = _iota((SPAN, LANES), 1) < DH_B
    strided = lambda start: pl.ds(start, SPAN, stride=d) if d > 1 else pl.ds(start, SPAN)
    for grp in range(0, d * nblk, BAND_UNROLL):
        blocks = [(idx // nblk, idx % nblk) for idx in range(grp, grp + BAND_UNROLL)]
        starts = [jb * reach + r for r, jb in blocks]
        qm, kk, vv = [], [], []
        for st in starts:
            q = qs_ref[strided(st), :]
            qm.append([jnp.where(sel, q, 0.0).astype(BF16) for sel in (low, ~low)])
            cur, prev = strided(BAND_TILE + st), strided(BAND_TILE + st - reach)
            kk.append(jnp.concatenate([ks_ref[prev, :], ks_ref[cur, :]], axis=0).astype(BF16))
            vv.append(jnp.concatenate([vs_ref[prev, :], vs_ref[cur, :]], axis=0).astype(BF16))
        scores = [[_nt(qh, kk_u) * DH_B ** -0.5 for qh in qm_u] for qm_u, kk_u in zip(qm, kk)]
        probs, dens, lses = [], [], []
        for (r, jb), s_u in zip(blocks, scores):
            valid = band & ((j >= SPAN) | (n > 0) | (jb > 0))
            p_u, d_u, l_u = [], [], []
            for s in s_u:
                s = jnp.where(valid, s, MASK_NEG)
                m = jnp.max(s, -1, keepdims=True)
                p = jnp.exp(s - m)
                den = jnp.sum(p, -1, keepdims=True)
                p_u.append(p.astype(BF16))
                d_u.append(den)
                l_u.append(m + jnp.log(den))
            probs.append(p_u)
            dens.append(d_u)
            lses.append(l_u)
        pvs = [[_dot(p, vv_u) for p in p_u] for p_u, vv_u in zip(probs, vv)]
        for st, pv_u, d_u, l_u in zip(starts, pvs, dens, lses):
            o_ref[strided(st), :] = jnp.where(low, pv_u[0] / d_u[0], pv_u[1] / d_u[1])
            l_ref[strided(st), :] = jnp.where(low, l_u[0], l_u[1])

    ks_ref[BAND_TILE - reach:BAND_TILE] = ks_ref[2 * BAND_TILE - reach:]
    vs_ref[BAND_TILE - reach:BAND_TILE] = vs_ref[2 * BAND_TILE - reach:]


def band_attn(proj, tables, g, bsz, t_len):
    d = B_DILATIONS[g]
    nt = t_len // BAND_TILE
    npair = W_B // LANES
    col = lambda j: pl.BlockSpec((BAND_TILE, LANES), lambda b, hp, n, j=j: (b * nt + n, j * npair + hp))
    tab = pl.BlockSpec((BAND_TILE, LANES), lambda b, hp, n: (n, 0))
    out = pl.BlockSpec((BAND_TILE, LANES), lambda b, hp, n: (b * nt + n, hp))
    return pl.pallas_call(
        functools.partial(_band_attn_kernel, d=d),
        grid=(bsz, npair, nt),
        in_specs=[col(4 + g), col(7 + g), col(10 + g), tab, tab, tab],
        out_specs=[out, out],
        out_shape=[jax.ShapeDtypeStruct((bsz * t_len, W_B), F32)] * 2,
        scratch_shapes=[pltpu.VMEM((BAND_TILE, LANES), F32), pltpu.VMEM((2 * BAND_TILE, LANES), F32),
                        pltpu.VMEM((2 * BAND_TILE, LANES), F32)],
        compiler_params=_params("parallel", "parallel", "arbitrary"),
        name=f"band_attn_d{d}",
    )(proj, proj, proj, *tables)


def _merge_kernel(o1, o2, o3, l1, l2, l3, g_ref, o_ref):
    la, lb_, lc = l1[...], l2[...], l3[...]
    m = jnp.maximum(jnp.maximum(la, lb_), lc)
    wa, wb, wc = jnp.exp(la - m), jnp.exp(lb_ - m), jnp.exp(lc - m)
    merged = (wa * o1[...] + wb * o2[...] + wc * o3[...]) / (wa + wb + wc)
    o_ref[...] = (merged * _silu(g_ref[...])).astype(BF16)


def merge_groups(outs, lses, proj):
    m = proj.shape[0]
    tm = 512
    row = pl.BlockSpec((tm, W_B), lambda i: (i, 0))
    return pl.pallas_call(
        _merge_kernel,
        grid=(m // tm,),
        in_specs=[row] * 6 + [pl.BlockSpec((tm, W_B), lambda i: (i, EVEN_N // W_B - 1))],
        out_specs=row,
        out_shape=jax.ShapeDtypeStruct((m, W_B), BF16),
        compiler_params=_params("parallel"),
        name="merge_groups",
    )(*outs, *lses, proj)


def _window_rows_kernel(k_ref, v_ref, cf_ref, sa_ref, sb_ref, o_ref):
    o_ref[0, :, :W_B] = _rope(k_ref[...], cf_ref[...], sa_ref[...], sb_ref[...], ROT_DIM_B // 2)
    o_ref[0, :, W_B:] = v_ref[...]


def window_rows(proj, tables, g, bsz, t_len):
    keep = min(B_WINDOWS[g], t_len)
    nk, nt = keep // SPAN, t_len // SPAN
    col = lambda j: pl.BlockSpec((SPAN, W_B), lambda b, i, j=j: (b * nt + nt - nk + i, j))
    tab = pl.BlockSpec((SPAN, LANES), lambda b, i: (nt - nk + i, 0))
    return pl.pallas_call(
        _window_rows_kernel,
        grid=(bsz, nk),
        in_specs=[col(7 + g), col(10 + g), tab, tab, tab],
        out_specs=pl.BlockSpec((1, SPAN, 2 * W_B), lambda b, i: (b, i, 0)),
        out_shape=jax.ShapeDtypeStruct((bsz, keep, 2 * W_B), F32),
        compiler_params=_params("parallel", "parallel"),
        name="window_rows",
    )(proj, proj, *tables)


def _step_attn_kernel(p_ref, c1_ref, c2_ref, c3_ref, cf_ref, sa_ref, sb_ref, o_ref, r1_ref, r2_ref, r3_ref, *, t_len):
    half = ROT_DIM_B // 2
    cf, sa, sb = cf_ref[...], sa_ref[...], sb_ref[...]
    nrow = H_B * t_len
    own_head = (_iota((nrow, W_B), 1) // DH_B) == (_iota((nrow, W_B), 0) // t_len)
    pad = jnp.zeros((LANES - t_len, W_B), F32)
    outs, lses = [], []
    for g, (c_ref, r_ref) in enumerate(((c1_ref, r1_ref), (c2_ref, r2_ref), (c3_ref, r3_ref))):
        d = B_DILATIONS[g]
        past = c_ref.shape[2]
        q = _rope(p_ref[:, (4 + g) * W_B:(5 + g) * W_B], cf, sa, sb, half)
        k_new = _rope(p_ref[:, (7 + g) * W_B:(8 + g) * W_B], cf, sa, sb, half)
        v_new = p_ref[:, (10 + g) * W_B:(11 + g) * W_B]
        r_ref[0, :, :W_B] = k_new
        r_ref[0, :, W_B:] = v_new
        qb = jnp.where(own_head, jnp.concatenate([q] * H_B, axis=0), 0.0).astype(BF16)
        s_c = _dot(qb, c_ref[0].astype(BF16)) * DH_B ** -0.5
        s_n = _nt(qb, jnp.concatenate([k_new, pad], axis=0).astype(BF16)) * DH_B ** -0.5
        dist = past + _iota(s_c.shape, 0) % t_len - _iota(s_c.shape, 1)
        s_c = jnp.where((dist % d == 0) & (dist // d <= SPAN), s_c, MASK_NEG)
        col = _iota(s_n.shape, 1)
        dist = _iota(s_n.shape, 0) % t_len - col
        s_n = jnp.where((col < t_len) & (dist >= 0) & (dist % d == 0) & (dist // d <= SPAN), s_n, MASK_NEG)
        m = jnp.maximum(jnp.max(s_c, -1, keepdims=True), jnp.max(s_n, -1, keepdims=True))
        e_c, e_n = jnp.exp(s_c - m), jnp.exp(s_n - m)
        den = jnp.sum(e_c, -1, keepdims=True) + jnp.sum(e_n, -1, keepdims=True)
        pv = _nt(e_c.astype(BF16), c_ref[1].astype(BF16))
        pv = pv + _dot(e_n.astype(BF16), jnp.concatenate([v_new, pad], axis=0).astype(BF16))
        outs.append(pv / den)
        lses.append(m + jnp.log(den))
    m = jnp.maximum(jnp.maximum(lses[0], lses[1]), lses[2])
    ws = [jnp.exp(l - m) for l in lses]
    merged = (ws[0] * outs[0] + ws[1] * outs[1] + ws[2] * outs[2]) / (ws[0] + ws[1] + ws[2])
    merged = jnp.where(own_head, merged, 0.0).reshape(H_B, t_len, W_B).sum(axis=0)
    o_ref[...] = (merged * _silu(p_ref[:, 13 * W_B:14 * W_B])).astype(BF16)


def step_attn(proj, caches, tables, e, bsz, t_len):
    views = [c.transpose(0, 1, 3, 4, 5, 2).reshape(c.shape[0], bsz, 2, W_B, c.shape[2]) for c in caches]
    specs = [pl.BlockSpec((None, None, 2, W_B, v.shape[4]), lambda b: (e, b, 0, 0, 0)) for v in views]
    tab = pl.BlockSpec((t_len, LANES), lambda b: (0, 0))
    rows = pl.BlockSpec((1, t_len, 2 * W_B), lambda b: (b, 0, 0))
    return pl.pallas_call(
        functools.partial(_step_attn_kernel, t_len=t_len),
        grid=(bsz,),
        in_specs=[pl.BlockSpec((t_len, EVEN_N), lambda b: (b, 0))] + specs + [tab, tab, tab],
        out_specs=[pl.BlockSpec((t_len, W_B), lambda b: (b, 0)), rows, rows, rows],
        out_shape=[jax.ShapeDtypeStruct((bsz * t_len, W_B), BF16)]
        + [jax.ShapeDtypeStruct((bsz, t_len, 2 * W_B), F32)] * 3,
        compiler_params=_params("parallel"),
        name="step_attn",
    )(proj, *views, *tables)


def _ssd_kernel(z_ref, x_ref, bc_ref, dt_ref, dtt_ref, cw_ref, cb_ref, dtb_ref, dtbc_ref, nega_ref, negac_ref,
                dskip_ref, ng_ref, expand_ref, conv0_ref, s0_ref,
                y_ref, convn_ref, sfin_ref, ext_ref, st_ref, acc_ref, *, chunk):
    c = pl.program_id(1)
    tail = CONV_W - 1

    @pl.when(c == 0)
    def _():
        ext_ref[0:8] = jnp.concatenate([jnp.zeros((8 - tail, CONV_DIM_C), F32), conv0_ref[0]], axis=0)
        st_ref[...] = s0_ref[0]

    ext_ref[8:8 + chunk, :D_INNER_C] = x_ref[...]
    ext_ref[8:8 + chunk, D_INNER_C:] = bc_ref[...]
    cw = cw_ref[...]
    conv = cb_ref[...] + sum(cw[j:j + 1, :] * ext_ref[8 - tail + j:8 - tail + j + chunk, :] for j in range(CONV_W))
    xbc = _silu(conv)

    @pl.when(c == pl.num_programs(1) - 1)
    def _():
        convn_ref[0] = ext_ref[8 + chunk - tail:8 + chunk, :]

    ext_ref[0:8] = ext_ref[chunk:chunk + 8]
    xs = xbc[:, :D_INNER_C]
    gn = G_C * N_C
    bm = xbc[:, D_INNER_C:D_INNER_C + gn].astype(BF16)
    cm = xbc[:, D_INNER_C + gn:].astype(BF16)

    softplus = lambda a: jnp.maximum(a, 0.0) + jnp.log1p(jnp.exp(-jnp.abs(a)))
    dt = softplus(dt_ref[...] + dtb_ref[...])
    la = dt * nega_ref[...]
    la_t = softplus(dtt_ref[0, 0] + dtbc_ref[...]) * negac_ref[...]
    ii = _iota((chunk, chunk), 0)
    jj = _iota((chunk, chunk), 1)
    causal = ii >= jj
    if chunk % 16 == 0:
        select = lambda sel, a, lhs: _dot_select(sel.astype(BF16), a, lhs)
    else:
        select = lambda sel, a, lhs: _dot_exact(sel.astype(F32), a) if lhs else _dot_exact(a, sel.astype(F32))
    b = select(causal, la, True)
    b_t = select(ii <= jj, la_t, False)
    expand = expand_ref[...]
    b_x = select(expand, b, False)
    dt_x = select(expand, dt, False)
    b_last = b_x[chunk - 1:chunk, :]
    xdt = xs * dt_x
    xdt_b = xdt.astype(BF16)
    xdtw = (xdt * jnp.exp(b_last - b_x)).astype(BF16)
    e_b = jnp.exp(b_x)
    low = _iota((chunk, LANES), 1) < P_C
    hg = H_C // G_C
    wg = hg * P_C
    for g in range(G_C):
        cg, bg = cm[:, g * N_C:(g + 1) * N_C], bm[:, g * N_C:(g + 1) * N_C]
        gs = slice(g * wg, (g + 1) * wg)
        acc_ref[:, gs] = _dot(cg, st_ref[:, gs].astype(BF16)) * e_b[:, gs]
        scores = _nt(cg, bg)
        for hp in range(hg // 2):
            sl = slice(g * wg + hp * LANES, g * wg + (hp + 1) * LANES)
            res = []
            for h in (g * hg + 2 * hp, g * hg + 2 * hp + 1):
                decay = jnp.exp(jnp.where(causal, b[:, h:h + 1] - b_t[h:h + 1, :], MASK_NEG))
                res.append(_dot((scores * decay).astype(BF16), xdt_b[:, sl]))
            acc_ref[:, sl] += jnp.where(low, res[0], res[1])
        st_ref[:, gs] = st_ref[:, gs] * jnp.exp(b_last[:, gs]) + _tn(bg, xdtw[:, gs])

    y = (acc_ref[...] + dskip_ref[...] * xs) * _silu(z_ref[...])
    for g in range(G_C):
        gs = slice(g * wg, (g + 1) * wg)
        yg = y[:, gs]
        y_ref[:, gs] = (yg * lax.rsqrt(jnp.mean(yg * yg, -1, keepdims=True) + RMS_EPS) * ng_ref[:, gs]).astype(BF16)

    @pl.when(c == pl.num_programs(1) - 1)
    def _():
        sfin_ref[0] = st_ref[...]


def ssd(proj, dt_t, prm, conv0_all, s0, o, bsz, t_len, chunk):
    nc = t_len // chunk
    blk = lambda w, j: pl.BlockSpec((chunk, w), lambda b, c, j=j: (b * nc + c, j))
    const = lambda a: pl.BlockSpec(a.shape, lambda b, c: (0,) * a.ndim)
    consts = [prm["conv_w"], prm["conv_b"], prm["dt_bias"], prm["dt_bias_col"], prm["neg_a"], prm["neg_a_col"],
              prm["d_skip"], prm["norm_g"], prm["expand"]]
    return pl.pallas_call(
        functools.partial(_ssd_kernel, chunk=chunk),
        grid=(bsz, nc),
        in_specs=[blk(D_INNER_C, 0), blk(D_INNER_C, 1), blk(2 * G_C * N_C, 4), blk(LANES, ODD_MAIN // LANES),
                  pl.BlockSpec((1, 1, H_C, chunk), lambda b, c: (b, c, 0, 0))]
        + [const(a) for a in consts]
        + [pl.BlockSpec((None, 1, CONV_W - 1, CONV_DIM_C), lambda b, c: (o, b, 0, 0)),
           pl.BlockSpec((1, N_C, D_INNER_C), lambda b, c: (b, 0, 0))],
        out_specs=[pl.BlockSpec((chunk, D_INNER_C), lambda b, c: (b * nc + c, 0)),
                   pl.BlockSpec((1, CONV_W - 1, CONV_DIM_C), lambda b, c: (b, 0, 0)),
                   pl.BlockSpec((1, N_C, D_INNER_C), lambda b, c: (b, 0, 0))],
        out_shape=[jax.ShapeDtypeStruct((bsz * t_len, D_INNER_C), BF16),
                   jax.ShapeDtypeStruct((bsz, CONV_W - 1, CONV_DIM_C), F32),
                   jax.ShapeDtypeStruct((bsz, N_C, D_INNER_C), F32)],
        scratch_shapes=[pltpu.VMEM((chunk + 8, CONV_DIM_C), F32), pltpu.VMEM((N_C, D_INNER_C), F32),
                        pltpu.VMEM((chunk, D_INNER_C), F32)],
        compiler_params=_params("parallel", "arbitrary"),
        name="ssd",
    )(proj, proj, proj, proj, dt_t, *consts, conv0_all, s0)


def _retention_kernel(q_ref, k_ref, v_ref, g_ref, cf_ref, sa_ref, sb_ref, dmat_ref, gq_ref, gw_ref, gs_ref, s0_ref,
                      o_ref, sfin_ref, st_ref, *, chunk):
    c = pl.program_id(1)

    nseq = s0_ref.shape[0]
    wst = H_D * DK_D

    @pl.when(c == 0)
    def _():
        for i in range(nseq):
            st_ref[i * wst:(i + 1) * wst, :] = s0_ref[i].reshape(wst, DV_D)

    half = DK_D // 2
    cf, sa, sb = cf_ref[...], sa_ref[...], sb_ref[...]
    gate = _silu(g_ref[...])
    low = _iota((chunk, LANES), 1) < DK_D
    items = [(i, h) for i in range(nseq) for h in range(H_D)]
    seq = lambda i: slice(i * chunk, (i + 1) * chunk)
    pair = lambda a, h: a[:, (h // 2) * LANES:(h // 2 + 1) * LANES]
    sel = lambda h: low if h % 2 == 0 else ~low
    qr = [_rope(q_ref[seq(i), :], cf, sa, sb, half) for i in range(nseq)]
    kr = [_rope(k_ref[seq(i), :], cf, sa, sb, half) * DK_D ** -0.5 for i in range(nseq)]
    vh = {(i, h): v_ref[seq(i), h * DV_D:(h + 1) * DV_D].astype(BF16) for i, h in items}
    states = {(i, hp): st_ref[i * wst + hp * LANES:i * wst + (hp + 1) * LANES, :]
              for i in range(nseq) for hp in range(H_D // 2)}
    qm = {(i, h): jnp.where(sel(h), pair(qr[i], h), 0.0).astype(BF16) for i, h in items}
    km = {(i, h): (jnp.where(sel(h), pair(kr[i], h), 0.0) * gw_ref[h]).astype(BF16) for i, h in items}
    scores = {(i, h): _nt(qm[i, h], pair(kr[i], h).astype(BF16)) for i, h in items}
    inter = {(i, h): _dot(qm[i, h], states[i, h // 2].astype(BF16)) for i, h in items}
    upd = {(i, h): _tn(km[i, h], vh[i, h]) for i, h in items}
    atts = {(i, h): (scores[i, h] * dmat_ref[h]).astype(BF16) for i, h in items}
    intra = {(i, h): _dot(atts[i, h], vh[i, h]) for i, h in items}
    for h in range(H_D):
        cols = []
        for i in range(nseq):
            oh = intra[i, h] + inter[i, h] * gq_ref[h]
            mu = jnp.mean(oh, -1, keepdims=True)
            cen = oh - mu
            var = jnp.mean(cen * cen, -1, keepdims=True)
            cols.append(cen * lax.rsqrt(var + LN_EPS))
        o_ref[:, h * DV_D:(h + 1) * DV_D] = (jnp.concatenate(cols, axis=0) * gate[:, h * DV_D:(h + 1) * DV_D]).astype(BF16)
    for (i, hp), s in states.items():
        st_ref[i * wst + hp * LANES:i * wst + (hp + 1) * LANES, :] = s * gs_ref[hp] + upd[i, 2 * hp] + upd[i, 2 * hp + 1]

    @pl.when(c == pl.num_programs(1) - 1)
    def _():
        for i in range(nseq):
            sfin_ref[i] = st_ref[i * wst:(i + 1) * wst, :].reshape(H_D, DK_D, DV_D)


def retention(proj, tables, consts, s0_all, o, bsz, t_len, chunk, nseq=1):
    nc = t_len // chunk
    assert nseq == 1 or nc == 1
    blk = lambda w, j: pl.BlockSpec((nseq * chunk, w), lambda b, c, j=j: (b * nc + c, j))
    tab = pl.BlockSpec((chunk, LANES), lambda b, c: (c, 0))
    const = lambda a: pl.BlockSpec(a.shape, lambda b, c: (0,) * a.ndim)
    wqk = H_D * DK_D
    return pl.pallas_call(
        functools.partial(_retention_kernel, chunk=chunk),
        grid=(bsz // nseq, nc),
        in_specs=[blk(wqk, 2560 // wqk), blk(wqk, 2816 // wqk), blk(W_D, 3072 // W_D), blk(W_D, 3584 // W_D),
                  tab, tab, tab] + [const(a) for a in consts]
        + [pl.BlockSpec((None, nseq, H_D, DK_D, DV_D), lambda b, c: (o, b, 0, 0, 0))],
        out_specs=[pl.BlockSpec((nseq * chunk, W_D), lambda b, c: (b * nc + c, 0)),
                   pl.BlockSpec((nseq, H_D, DK_D, DV_D), lambda b, c: (b, 0, 0, 0))],
        out_shape=[jax.ShapeDtypeStruct((bsz * t_len, W_D), BF16),
                   jax.ShapeDtypeStruct((bsz, H_D, DK_D, DV_D), F32)],
        scratch_shapes=[pltpu.VMEM((nseq * H_D * DK_D, DV_D), F32)],
        compiler_params=_params("parallel", "arbitrary"),
        name="retention",
    )(proj, proj, proj, proj, *tables, *consts, s0_all)


def _retention_consts(chunk):
    lg = np.log1p(-np.exp2(-5.0 - np.arange(H_D, dtype=np.float64)))
    t = np.arange(chunk)
    diff = t[:, None] - t[None, :]
    dmat = np.where(diff >= 0, np.exp(lg[:, None, None] * diff), 0.0)
    gq = np.broadcast_to(np.exp(lg[:, None, None] * (t[None, :, None] + 1)), (H_D, chunk, DV_D))
    gw = np.broadcast_to(np.exp(lg[:, None, None] * (chunk - 1 - t[None, :, None])), (H_D, chunk, LANES))
    gs = np.repeat(np.exp(lg * chunk), DK_D).reshape(H_D // 2, 2 * DK_D, 1)
    gs = np.broadcast_to(gs, (H_D // 2, 2 * DK_D, DV_D))
    return [jnp.asarray(a, F32) for a in (dmat, gq, gw, gs)]


MEM_LANE_TILES = DH_M // LANES
MEM_ROW_PERIOD = 2 * MEM_LANE_TILES * H_M


def mem_kv_rows(kv):
    lead = kv.shape[:-4]
    kv = kv.reshape(*lead, N_MEM, 2, H_M, MEM_LANE_TILES, LANES)
    return jnp.swapaxes(kv, -3, -2).reshape(*lead, N_MEM * MEM_ROW_PERIOD, LANES)


def _mem_attn_kernel(q_ref, kv_ref, o_ref):
    nseq = kv_ref.shape[0]
    rows = q_ref.shape[0] // nseq

    def head_rows(i, which, h):
        tiles = [kv_ref[i, pl.ds((which * MEM_LANE_TILES + j) * H_M + h, N_MEM, stride=MEM_ROW_PERIOD), :]
                 for j in range(MEM_LANE_TILES)]
        return jnp.concatenate(tiles, axis=1).astype(BF16)

    items = [(i, h) for i in range(nseq) for h in range(H_M)]
    block = lambda i, h: (slice(i * rows, (i + 1) * rows), slice(h * DH_M, (h + 1) * DH_M))
    scores = [_nt(q_ref[block(i, h)].astype(BF16), head_rows(i, 0, h)) * DH_M ** -0.5 for i, h in items]
    probs, dens = [], []
    for s in scores:
        p = jnp.exp(s - jnp.max(s, -1, keepdims=True))
        dens.append(jnp.sum(p, -1, keepdims=True))
        probs.append(p.astype(BF16))
    outs = [_dot(p, head_rows(i, 1, h)) for (i, h), p in zip(items, probs)]
    normed = {item: o / den for item, o, den in zip(items, outs, dens)}
    for h in range(H_M):
        col = jnp.concatenate([normed[(i, h)] for i in range(nseq)], axis=0)
        o_ref[:, h * DH_M:(h + 1) * DH_M] = col.astype(BF16)


MEM_SEQS_PER_STEP = 4


def mem_attn(q, kv, layer, bsz, t_len):
    tq = min(t_len, 512)
    nq = t_len // tq
    nseq = MEM_SEQS_PER_STEP if (nq == 1 and tq * MEM_SEQS_PER_STEP <= 512 and bsz % MEM_SEQS_PER_STEP == 0) else 1
    tq *= nseq
    return pl.pallas_call(
        _mem_attn_kernel,
        grid=(bsz // nseq, nq),
        in_specs=[pl.BlockSpec((tq, D_MODEL), lambda b, i: (b * nq + i, 0)),
                  pl.BlockSpec((None, nseq, N_MEM * MEM_ROW_PERIOD, LANES), lambda b, i: (layer, b, 0, 0))],
        out_specs=pl.BlockSpec((tq, D_MODEL), lambda b, i: (b * nq + i, 0)),
        out_shape=jax.ShapeDtypeStruct((bsz * t_len, D_MODEL), BF16),
        compiler_params=_params("parallel", "arbitrary"),
        name="mem_attn",
    )(q, kv)


def _even_layer(x, xb, e, w_in, w_out, lb, gn, ln_g, ln_b, state_a, bsz, t_len, tables, caches):
    proj = matmul(xb, w_in, 512)
    if caches is None:
        o_a, s_a = hgrn2(proj, lb, gn, state_a, 0, bsz, t_len, GLA_CHUNK, GLA_SUB, GLA_ROWS)
        res = [band_attn(proj, tables, g, bsz, t_len) for g in range(N_BG)]
        o_b = merge_groups([r[0] for r in res], [r[1] for r in res], proj)
        rows = [window_rows(proj, tables, g, bsz, t_len) for g in range(N_BG)]
    else:
        o_a, s_a = hgrn2(proj, lb, gn, state_a, e, bsz, t_len, t_len, t_len, t_len, STEP_SEQS if bsz % STEP_SEQS == 0 else 1)
        o_b, *rows = step_attn(proj, caches, tables, e, bsz, t_len)
    x, xb = outproj_ln([o_a, o_b], [w_out[:W_A], w_out[W_A:]], x, ln_g, ln_b)
    rows = [r.reshape(bsz, r.shape[1], 2, H_B, DH_B) for r in rows]
    return x, xb, s_a, rows


def _odd_layer(x, xb, o, w_in, w_out, prm, ln_g, ln_b, s_c0, conv0, s_d0, bsz, t_len, chunk, tables, ret_consts):
    proj = matmul(xb, w_in, 384)
    nc = t_len // chunk
    dt_t = proj[:, ODD_MAIN:ODD_MAIN + H_C].reshape(bsz, nc, chunk, H_C).transpose(0, 1, 3, 2)
    y_c, conv_n, s_c = ssd(proj, dt_t, prm, conv0, s_c0, o, bsz, t_len, chunk)
    nseq = STEP_SEQS if (nc == 1 and bsz % STEP_SEQS == 0) else 1
    o_d, s_d = retention(proj, tables, ret_consts, s_d0, o, bsz, t_len, chunk, nseq)
    x, xb = outproj_ln([y_c, o_d], [w_out[:D_INNER_C], w_out[D_INNER_C:]], x, ln_g, ln_b)
    return x, xb, s_c, conv_n, s_d


def _memory_layer(x, xb, layer, wq, wo, kv, ln_g, ln_b, bsz, t_len):
    q = matmul(xb, wq, 512)
    att = mem_attn(q, kv, layer, bsz, t_len)
    return outproj_ln([att], [wo], x, ln_g, ln_b)


def _state_to_lanes(s):
    return s.transpose(0, 2, 1, 3).reshape(s.shape[0], N_C, D_INNER_C)


def _state_from_lanes(s):
    return s.reshape(s.shape[0], N_C, H_C, P_C).transpose(0, 2, 1, 3)


def kernel(x_prompt, x_sample, state_a, cache_b1, cache_b2, cache_b3, state_c_ssm, state_c_conv, state_d,
           cache_mem_kv, mem_prompt, w_in_even, a_lb_logits, a_norm_g, w_out_even, w_in_odd, c_conv_w, c_conv_b,
           c_dt_bias, c_a_log, c_d_skip, c_norm_g, w_out_odd, ln1_g, ln1_b, ln2_g, ln2_b, m_wq, m_wkv, m_wo):
    bp, tp = x_prompt.shape[0], x_prompt.shape[1]
    bs, ts = x_sample.shape[0], x_sample.shape[1]
    pos_p = jnp.arange(tp, dtype=jnp.int32)
    pos_s = PAST_LEN + jnp.arange(ts, dtype=jnp.int32)
    tab_b = {"p": _rope_tables(pos_p, ROT_DIM_B, DH_B, ROPE_THETA), "s": _rope_tables(pos_s, ROT_DIM_B, DH_B, ROPE_THETA)}
    tab_d = {"p": _rope_tables(pos_p, DK_D, DK_D, RET_THETA), "s": _rope_tables(pos_s, DK_D, DK_D, RET_THETA)}
    ret_consts = {"p": _retention_consts(RET_CHUNK), "s": _retention_consts(ts)}

    sm = jax.nn.softmax(a_lb_logits.astype(F32), axis=0)
    lb_all = jnp.cumsum(sm, axis=0) - sm[0]
    expand = jnp.asarray(np.pad(np.repeat(np.eye(H_C), P_C, axis=1), ((0, LANES - H_C), (0, 0))), F32)
    pad_h = lambda a: jnp.pad(a.astype(F32), (0, LANES - H_C))[None, :]
    dt0 = D_INNER_C + CONV_DIM_C
    zeros_a = jnp.zeros((1, bp, H_A, DK_A, DV_A), F32)
    zeros_c = jnp.zeros((bp, N_C, D_INNER_C), F32)
    zeros_conv = jnp.zeros((1, bp, CONV_W - 1, CONV_DIM_C), F32)
    zeros_d = jnp.zeros((1, bp, H_D, DK_D, DV_D), F32)
    caches = (cache_b1, cache_b2, cache_b3)
    mem_kv_s = mem_kv_rows(cache_mem_kv)

    xp, xs = x_prompt.reshape(bp * tp, D_MODEL), x_sample.reshape(bs * ts, D_MODEL)
    xpb, xsb = xp.astype(BF16), xs.astype(BF16)
    mem_b = mem_prompt.reshape(bp * N_MEM, D_MODEL).astype(BF16)
    a_p, a_s, c_p, c_s, cv_p, cv_s, d_p, d_s, mem_p = [], [], [], [], [], [], [], [], []
    b_p = [[] for _ in range(N_BG)]
    b_s = [[] for _ in range(N_BG)]
    for l in range(DEPTH):
        if l % 2 == 0:
            e = l // 2
            w_in, w_out = w_in_even[e].astype(BF16), w_out_even[e].astype(BF16)
            lb = lb_all[e][None, :]
            gn = jnp.tile(a_norm_g[e].astype(F32), H_A)[None, :]
            g1, b1 = ln1_g[l][None, :], ln1_b[l][None, :]
            xp, xpb, sap, rp = _even_layer(xp, xpb, e, w_in, w_out, lb, gn, g1, b1, zeros_a, bp, tp, tab_b["p"], None)
            xs, xsb, sas, rs = _even_layer(xs, xsb, e, w_in, w_out, lb, gn, g1, b1, state_a, bs, ts, tab_b["s"], caches)
            a_p.append(sap)
            a_s.append(sas)
            for g in range(N_BG):
                b_p[g].append(rp[g])
                b_s[g].append(rs[g])
        else:
            o = l // 2
            w = w_in_odd[o]
            w_in = jnp.concatenate([w[:, :dt0], w[:, dt0 + H_C:], w[:, dt0:dt0 + H_C],
                                    jnp.zeros((D_MODEL, LANES - H_C), w.dtype)], axis=1).astype(BF16)
            w_out = w_out_odd[o].astype(BF16)
            prm = {
                "conv_w": c_conv_w[o].astype(F32), "conv_b": c_conv_b[o].astype(F32)[None, :],
                "dt_bias": pad_h(c_dt_bias[o]), "dt_bias_col": c_dt_bias[o].astype(F32)[:, None],
                "neg_a": pad_h(-jnp.exp(c_a_log[o].astype(F32))), "neg_a_col": -jnp.exp(c_a_log[o].astype(F32))[:, None],
                "d_skip": jnp.repeat(c_d_skip[o].astype(F32), P_C)[None, :], "norm_g": c_norm_g[o].astype(F32)[None, :],
                "expand": expand,
            }
            g1, b1 = ln1_g[l][None, :], ln1_b[l][None, :]
            xp, xpb, scp, cvp, sdp = _odd_layer(xp, xpb, 0, w_in, w_out, prm, g1, b1, zeros_c, zeros_conv, zeros_d,
                                                bp, tp, SSD_CHUNK, tab_d["p"], ret_consts["p"])
            xs, xsb, scs, cvs, sds = _odd_layer(xs, xsb, o, w_in, w_out, prm, g1, b1, _state_to_lanes(state_c_ssm[o]),
                                                state_c_conv, state_d, bs, ts, ts, tab_d["s"], ret_consts["s"])
            c_p.append(_state_from_lanes(scp))
            c_s.append(_state_from_lanes(scs))
            cv_p.append(cvp)
            cv_s.append(cvs)
            d_p.append(sdp)
            d_s.append(sds)
        kv_p = matmul(mem_b, m_wkv[l].astype(BF16), 512)
        kv_p = kv_p.reshape(bp, N_MEM, 2, H_M, DH_M)
        mem_p.append(kv_p)
        wq, wo = m_wq[l].astype(BF16), m_wo[l].astype(BF16)
        g2, b2 = ln2_g[l][None, :], ln2_b[l][None, :]
        xp, xpb = _memory_layer(xp, xpb, 0, wq, wo, mem_kv_rows(kv_p)[None], g2, b2, bp, tp)
        xs, xsb = _memory_layer(xs, xsb, l, wq, wo, mem_kv_s, g2, b2, bs, ts)
    return (xp.reshape(bp, tp, D_MODEL), xs.reshape(bs, ts, D_MODEL), jnp.stack(a_p), jnp.stack(a_s),
            jnp.stack(b_p[0]), jnp.stack(b_p[1]), jnp.stack(b_p[2]),
            jnp.stack(b_s[0]), jnp.stack(b_s[1]), jnp.stack(b_s[2]),
            jnp.stack(c_p), jnp.stack(c_s), jnp.stack(cv_p), jnp.stack(cv_s),
            jnp.stack(d_p), jnp.stack(d_s), jnp.stack(mem_p))
```

```python
import functools
import math

import numpy as np
import jax
import jax.numpy as jnp
from jax import lax
from jax.experimental import pallas as pl
from jax.experimental.pallas import tpu as pltpu

F32 = jnp.float32
BF16 = jnp.bfloat16
HIGHEST = lax.Precision.HIGHEST

D_MODEL = 1024
DEPTH = 4
PAST_LEN = 2048
H_A, DK_A, DV_A = 4, 128, 128
W_A = H_A * DV_A
GLA_CHUNK = 64
GLA_SUB = 16
GLA_ROWS = 512
STEP_SEQS = 4
B_WINDOWS = (128, 512, 2048)
B_DILATIONS = (1, 4, 16)
N_BG, H_B, DH_B = 3, 8, 64
W_B = H_B * DH_B
SPAN = 128
ROT_DIM_B = DH_B // 4
ROPE_THETA = 500000.0
H_C, P_C, G_C, N_C, CONV_W = 16, 64, 2, 128, 4
D_INNER_C = H_C * P_C
CONV_DIM_C = D_INNER_C + 2 * G_C * N_C
SSD_CHUNK = 128
H_D, DK_D, DV_D = 4, 64, 128
W_D = H_D * DV_D
RET_CHUNK = 128
RET_THETA = 10000.0
H_M = 4
DH_M = D_MODEL // H_M
N_MEM = 256

ALPHA = (2.0 * DEPTH) ** 0.25
LN_EPS = 1e-5
RMS_EPS = 1e-6
MASK_NEG = -1e30

LANES = 128
EVEN_N = 7168
ODD_MAIN = 4096
ODD_N = ODD_MAIN + LANES
VMEM_LIMIT = 56 * 1024 * 1024


def _params(*sem):
    return pltpu.CompilerParams(dimension_semantics=sem, vmem_limit_bytes=VMEM_LIMIT)


def _nt(a, b):
    return lax.dot_general(a, b, (((1,), (1,)), ((), ())), preferred_element_type=F32)


def _tn(a, b):
    return lax.dot_general(a, b, (((0,), (0,)), ((), ())), preferred_element_type=F32)


def _dot(a, b):
    return jnp.dot(a, b, preferred_element_type=F32)


def _dot_exact(a, b):
    return jnp.dot(a, b, preferred_element_type=F32, precision=HIGHEST)


def _dot_select(sel, x, sel_is_lhs):
    hi = x.astype(BF16)
    rest = x - hi.astype(F32)
    mid = rest.astype(BF16)
    lo = (rest - mid.astype(F32)).astype(BF16)
    mm = (lambda t: _dot(sel, t)) if sel_is_lhs else (lambda t: _dot(t, sel))
    return mm(hi) + mm(mid) + mm(lo)


def _silu(x):
    return x * jax.nn.sigmoid(x)


def _iota(shape, dim):
    return lax.broadcasted_iota(jnp.int32, shape, dim)


def _rope(x, cf, sa, sb, shift):
    outs = []
    for j in range(x.shape[1] // LANES):
        xj = x[:, j * LANES:(j + 1) * LANES]
        outs.append(xj * cf + pltpu.roll(xj, shift, 1) * sa + pltpu.roll(xj, LANES - shift, 1) * sb)
    return outs[0] if len(outs) == 1 else jnp.concatenate(outs, axis=1)


def _rope_tables(pos, rot_dim, head_dim, theta):
    half = rot_dim // 2
    inv_freq = theta ** (-jnp.arange(half, dtype=F32) / half)
    ang = pos.astype(F32)[:, None] * inv_freq
    cos, sin = jnp.cos(ang), jnp.sin(ang)
    lane = np.arange(LANES) % head_dim
    idx = lane % half
    cf = jnp.where(lane < rot_dim, cos[:, idx], 1.0)
    sa = jnp.where((lane >= half) & (lane < rot_dim), sin[:, idx], 0.0)
    sb = jnp.where(lane < half, -sin[:, idx], 0.0)
    return cf.astype(F32), sa.astype(F32), sb.astype(F32)


def _mm_kernel(x_ref, w_ref, o_ref):
    o_ref[...] = _dot(x_ref[...], w_ref[...])


def matmul(x, w, tn):
    m, k = x.shape
    n = w.shape[1]
    tm = min(m, 2048)
    return pl.pallas_call(
        _mm_kernel,
        grid=(m // tm, n // tn),
        in_specs=[pl.BlockSpec((tm, k), lambda i, j: (i, 0)), pl.BlockSpec((k, tn), lambda i, j: (0, j))],
        out_specs=pl.BlockSpec((tm, tn), lambda i, j: (i, j)),
        out_shape=jax.ShapeDtypeStruct((m, n), F32),
        compiler_params=_params("parallel", "parallel"),
        name="matmul",
    )(x, w)


def _outproj_ln_kernel(*refs, n_parts):
    parts, ws = refs[:n_parts], refs[n_parts:2 * n_parts]
    x_ref, g_ref, b_ref, o_ref, ob_ref = refs[2 * n_parts:]
    acc = ALPHA * x_ref[...]
    for p_ref, w_ref in zip(parts, ws):
        acc = acc + _dot(p_ref[...], w_ref[...])
    mu = jnp.mean(acc, -1, keepdims=True)
    cen = acc - mu
    var = jnp.mean(cen * cen, -1, keepdims=True)
    y = cen * lax.rsqrt(var + LN_EPS) * g_ref[...] + b_ref[...]
    o_ref[...] = y
    ob_ref[...] = y.astype(BF16)


def outproj_ln(parts, ws, x, g, b):
    m = x.shape[0]
    tm = min(m, 512)
    row = lambda w: pl.BlockSpec((tm, w), lambda i: (i, 0))
    const = lambda a: pl.BlockSpec(a.shape, lambda i: (0, 0))
    return pl.pallas_call(
        functools.partial(_outproj_ln_kernel, n_parts=len(parts)),
        grid=(m // tm,),
        in_specs=[row(p.shape[1]) for p in parts] + [const(w) for w in ws] + [row(D_MODEL), const(g), const(b)],
        out_specs=[row(D_MODEL), row(D_MODEL)],
        out_shape=[jax.ShapeDtypeStruct((m, D_MODEL), F32), jax.ShapeDtypeStruct((m, D_MODEL), BF16)],
        compiler_params=_params("parallel"),
        name="outproj_ln",
    )(*parts, *ws, x, g, b)


def _hgrn2_kernel(q_ref, f_ref, v_ref, g_ref, lb_ref, gn_ref, s0_ref, o_ref, sfin_ref, st_ref, *, chunk, sub):
    c = pl.program_id(1)
    nseq = s0_ref.shape[0]
    seq_rows = q_ref.shape[0] // nseq

    @pl.when(c == 0)
    def _():
        for i in range(nseq):
            for h in range(H_A):
                st_ref[i * H_A + h] = s0_ref[i, h].T

    lb = lb_ref[...]
    gn = gn_ref[...]
    starts = range(0, q_ref.shape[0], chunk)
    nsub = chunk // sub
    tril = _iota((chunk, chunk), 0) >= _iota((chunk, chunk), 1)
    if chunk % 16 == 0:
        cumsum = lambda a: _dot_select(tril.astype(BF16), a, True)
    else:
        cumsum = lambda a: _dot_exact(tril.astype(F32), a)
    head = lambda a, h: a[:, h * DK_A:(h + 1) * DK_A]
    lane_head = _iota((H_A * sub, W_A), 1) // DK_A == _iota((H_A * sub, W_A), 0) // sub

    fx = f_ref[...]
    log_f = jnp.log(lb + (1.0 - lb) * jax.nn.sigmoid(fx))
    k = (1.0 - lb) * jax.nn.sigmoid(-fx)
    q = _silu(q_ref[...])
    v = v_ref[...].astype(BF16)
    bs = [cumsum(log_f[c0:c0 + chunk]) for c0 in starts]

    q_in, g_last, kw, q_bd, kt = [], [], [], [], []
    for c0, b in zip(starts, bs):
        qc, kc = q[c0:c0 + chunk], k[c0:c0 + chunk]
        b_last = b[chunk - 1:chunk, :]
        q_in.append((qc * jnp.exp(b)).astype(BF16))
        g_last.append(jnp.exp(b_last))
        kw.append((kc * jnp.exp(b_last - b)).astype(BF16))
        for i in range(nsub):
            r0, r1 = i * sub, (i + 1) * sub
            ref = b[r0 - 1:r0, :] if i > 0 else jnp.zeros((1, W_A), F32)
            qt = qc[r0:r1] * jnp.exp(b[r0:r1] - ref)
            q_bd.append(jnp.where(lane_head, jnp.concatenate([qt] * H_A, axis=0), 0.0).astype(BF16))
            kt.append((kc[:r1] * jnp.exp(ref - b[:r1])).astype(BF16))
    upd = [[_tn(v[c0:c0 + chunk, h * DV_A:(h + 1) * DV_A], head(kw_c, h)) for h in range(H_A)]
           for c0, kw_c in zip(starts, kw)]
    scores = [_nt(a, b_) for a, b_ in zip(q_bd, kt)]
    atts = []
    for idx, s in enumerate(scores):
        r0 = (idx % nsub) * sub
        causal = _iota(s.shape, 1) <= _iota(s.shape, 0) % sub + r0
        atts.append(jnp.where(causal, s, 0.0).astype(BF16))
    intra = []
    for idx, att in enumerate(atts):
        c0 = starts[idx // nsub]
        pv = _dot(att, v[c0:c0 + att.shape[1]])
        intra.append([pv[h * sub:(h + 1) * sub, h * DV_A:(h + 1) * DV_A] for h in range(H_A)])

    gate = _silu(g_ref[...])
    state = [st_ref[j] for j in range(nseq * H_A)]
    outs = [[] for _ in range(H_A)]
    for ci, c0 in enumerate(starts):
        for h in range(H_A):
            j = (c0 // seq_rows) * H_A + h
            oh = _nt(head(q_in[ci], h), state[j].astype(BF16))
            oh = oh + jnp.concatenate([intra[ci * nsub + i][h] for i in range(nsub)], axis=0)
            outs[h].append(oh * lax.rsqrt(jnp.mean(oh * oh, -1, keepdims=True) + RMS_EPS) * head(gn, h))
            state[j] = state[j] * head(g_last[ci], h) + upd[ci][h]
    for h in range(H_A):
        o_ref[:, h * DV_A:(h + 1) * DV_A] = (jnp.concatenate(outs[h], axis=0) * head(gate, h)).astype(BF16)
    for j in range(nseq * H_A):
        st_ref[j] = state[j]

    @pl.when(c == pl.num_programs(1) - 1)
    def _():
        for i in range(nseq):
            for h in range(H_A):
                sfin_ref[i, h] = st_ref[i * H_A + h].T


def hgrn2(proj, lb, gn, s0_all, e, bsz, t_len, chunk, sub, rows, nseq=1):
    nc = t_len // rows
    assert nseq == 1 or nc == 1
    col = lambda j: pl.BlockSpec((nseq * rows, W_A), lambda b, c, j=j: (b * nc + c, j))
    const = pl.BlockSpec((1, W_A), lambda b, c: (0, 0))
    state = pl.BlockSpec((None, nseq, H_A, DK_A, DV_A), lambda b, c: (e, b, 0, 0, 0))
    return pl.pallas_call(
        functools.partial(_hgrn2_kernel, chunk=chunk, sub=sub),
        grid=(bsz // nseq, nc),
        in_specs=[col(0), col(1), col(2), col(3), const, const, state],
        out_specs=[pl.BlockSpec((nseq * rows, W_A), lambda b, c: (b * nc + c, 0)),
                   pl.BlockSpec((nseq, H_A, DK_A, DV_A), lambda b, c: (b, 0, 0, 0))],
        out_shape=[jax.ShapeDtypeStruct((bsz * t_len, W_A), BF16),
                   jax.ShapeDtypeStruct((bsz, H_A, DK_A, DV_A), F32)],
        scratch_shapes=[pltpu.VMEM((nseq * H_A, DV_A, DK_A), F32)],
        compiler_params=_params("parallel", "arbitrary"),
        name="hgrn2",
    )(proj, proj, proj, proj, lb, gn, s0_all)


BAND_TILE = SPAN * max(B_DILATIONS)


BAND_UNROLL = 4


def _band_attn_kernel(q_ref, k_ref, v_ref, cf_ref, sa_ref, sb_ref, o_ref, l_ref, qs_ref, ks_ref, vs_ref, *, d):
    n = pl.program_id(2)
    half = ROT_DIM_B // 2
    reach = SPAN * d
    nblk = BAND_TILE // reach

    @pl.when(n == 0)
    def _():
        ks_ref[:BAND_TILE] = jnp.zeros((BAND_TILE, LANES), F32)
        vs_ref[:BAND_TILE] = jnp.zeros((BAND_TILE, LANES), F32)

    cf, sa, sb = cf_ref[...], sa_ref[...], sb_ref[...]
    qs_ref[...] = _rope(q_ref[...], cf, sa, sb, half)
    ks_ref[BAND_TILE:] = _rope(k_ref[...], cf, sa, sb, half)
    vs_ref[BAND_TILE:] = v_ref[...]

    i = _iota((SPAN, 2 * SPAN), 0)
    j = _iota((SPAN, 2 * SPAN), 1)
    dist = i + SPAN - j
    band = (dist >= 0) & (dist <= SPAN)
    low = _iota((SPAN, LANES), 1) < DH_B
    strided = lambda start: pl.ds(start, SPAN, stride=d) if d > 1 else pl.ds(start, SPAN)
    for grp in range(0, d * nblk, BAND_UNROLL):
        blocks = [(idx // nblk, idx % nblk) for idx in range(grp, grp + BAND_UNROLL)]
        starts = [jb * reach + r for r, jb in blocks]
        qm, kk, vv = [], [], []
        for st in starts:
            q = qs_ref[strided(st), :]
            qm.append([jnp.where(sel, q, 0.0).astype(BF16) for sel in (low, ~low)])
            cur, prev = strided(BAND_TILE + st), strided(BAND_TILE + st - reach)
            kk.append(jnp.concatenate([ks_ref[prev, :], ks_ref[cur, :]], axis=0).astype(BF16))
            vv.append(jnp.concatenate([vs_ref[prev, :], vs_ref[cur, :]], axis=0).astype(BF16))
        scores = [[_nt(qh, kk_u) * DH_B ** -0.5 for qh in qm_u] for qm_u, kk_u in zip(qm, kk)]
        probs, dens, lses = [], [], []
        for (r, jb), s_u in zip(blocks, scores):
            valid = band & ((j >= SPAN) | (n > 0) | (jb > 0))
            p_u, d_u, l_u = [], [], []
            for s in s_u:
                s = jnp.where(valid, s, MASK_NEG)
                m = jnp.max(s, -1, keepdims=True)
                p = jnp.exp(s - m)
                den = jnp.sum(p, -1, keepdims=True)
                p_u.append(p.astype(BF16))
                d_u.append(den)
                l_u.append(m + jnp.log(den))
            probs.append(p_u)
            dens.append(d_u)
            lses.append(l_u)
        pvs = [[_dot(p, vv_u) for p in p_u] for p_u, vv_u in zip(probs, vv)]
        for st, pv_u, d_u, l_u in zip(starts, pvs, dens, lses):
            o_ref[strided(st), :] = jnp.where(low, pv_u[0] / d_u[0], pv_u[1] / d_u[1])
            l_ref[strided(st), :] = jnp.where(low, l_u[0], l_u[1])

    ks_ref[BAND_TILE - reach:BAND_TILE] = ks_ref[2 * BAND_TILE - reach:]
    vs_ref[BAND_TILE - reach:BAND_TILE] = vs_ref[2 * BAND_TILE - reach:]


def band_attn(proj, tables, g, bsz, t_len):
    d = B_DILATIONS[g]
    nt = t_len // BAND_TILE
    npair = W_B // LANES
    col = lambda j: pl.BlockSpec((BAND_TILE, LANES), lambda b, hp, n, j=j: (b * nt + n, j * npair + hp))
    tab = pl.BlockSpec((BAND_TILE, LANES), lambda b, hp, n: (n, 0))
    out = pl.BlockSpec((BAND_TILE, LANES), lambda b, hp, n: (b * nt + n, hp))
    return pl.pallas_call(
        functools.partial(_band_attn_kernel, d=d),
        grid=(bsz, npair, nt),
        in_specs=[col(4 + g), col(7 + g), col(10 + g), tab, tab, tab],
        out_specs=[out, out],
        out_shape=[jax.ShapeDtypeStruct((bsz * t_len, W_B), F32)] * 2,
        scratch_shapes=[pltpu.VMEM((BAND_TILE, LANES), F32), pltpu.VMEM((2 * BAND_TILE, LANES), F32),
                        pltpu.VMEM((2 * BAND_TILE, LANES), F32)],
        compiler_params=_params("parallel", "parallel", "arbitrary"),
        name=f"band_attn_d{d}",
    )(proj, proj, proj, *tables)


def _merge_kernel(o1, o2, o3, l1, l2, l3, g_ref, o_ref):
    la, lb_, lc = l1[...], l2[...], l3[...]
    m = jnp.maximum(jnp.maximum(la, lb_), lc)
    wa, wb, wc = jnp.exp(la - m), jnp.exp(lb_ - m), jnp.exp(lc - m)
    merged = (wa * o1[...] + wb * o2[...] + wc * o3[...]) / (wa + wb + wc)
    o_ref[...] = (merged * _silu(g_ref[...])).astype(BF16)


def merge_groups(outs, lses, proj):
    m = proj.shape[0]
    tm = 512
    row = pl.BlockSpec((tm, W_B), lambda i: (i, 0))
    return pl.pallas_call(
        _merge_kernel,
        grid=(m // tm,),
        in_specs=[row] * 6 + [pl.BlockSpec((tm, W_B), lambda i: (i, EVEN_N // W_B - 1))],
        out_specs=row,
        out_shape=jax.ShapeDtypeStruct((m, W_B), BF16),
        compiler_params=_params("parallel"),
        name="merge_groups",
    )(*outs, *lses, proj)


def _window_rows_kernel(k_ref, v_ref, cf_ref, sa_ref, sb_ref, o_ref):
    o_ref[0, :, :W_B] = _rope(k_ref[...], cf_ref[...], sa_ref[...], sb_ref[...], ROT_DIM_B // 2)
    o_ref[0, :, W_B:] = v_ref[...]


def window_rows(proj, tables, g, bsz, t_len):
    keep = min(B_WINDOWS[g], t_len)
    nk, nt = keep // SPAN, t_len // SPAN
    col = lambda j: pl.BlockSpec((SPAN, W_B), lambda b, i, j=j: (b * nt + nt - nk + i, j))
    tab = pl.BlockSpec((SPAN, LANES), lambda b, i: (nt - nk + i, 0))
    return pl.pallas_call(
        _window_rows_kernel,
        grid=(bsz, nk),
        in_specs=[col(7 + g), col(10 + g), tab, tab, tab],
        out_specs=pl.BlockSpec((1, SPAN, 2 * W_B), lambda b, i: (b, i, 0)),
        out_shape=jax.ShapeDtypeStruct((bsz, keep, 2 * W_B), F32),
        compiler_params=_params("parallel", "parallel"),
        name="window_rows",
    )(proj, proj, *tables)


def _step_attn_kernel(p_ref, c1_ref, c2_ref, c3_ref, cf_ref, sa_ref, sb_ref, o_ref, r1_ref, r2_ref, r3_ref, *, t_len):
    half = ROT_DIM_B // 2
    cf, sa, sb = cf_ref[...], sa_ref[...], sb_ref[...]
    nrow = H_B * t_len
    own_head = (_iota((nrow, W_B), 1) // DH_B) == (_iota((nrow, W_B), 0) // t_len)
    pad = jnp.zeros((LANES - t_len, W_B), F32)
    outs, lses = [], []
    for g, (c_ref, r_ref) in enumerate(((c1_ref, r1_ref), (c2_ref, r2_ref), (c3_ref, r3_ref))):
        d = B_DILATIONS[g]
        past = c_ref.shape[2]
        q = _rope(p_ref[:, (4 + g) * W_B:(5 + g) * W_B], cf, sa, sb, half)
        k_new = _rope(p_ref[:, (7 + g) * W_B:(8 + g) * W_B], cf, sa, sb, half)
        v_new = p_ref[:, (10 + g) * W_B:(11 + g) * W_B]
        r_ref[0, :, :W_B] = k_new
        r_ref[0, :, W_B:] = v_new
        qb = jnp.where(own_head, jnp.concatenate([q] * H_B, axis=0), 0.0).astype(BF16)
        s_c = _dot(qb, c_ref[0].astype(BF16)) * DH_B ** -0.5
        s_n = _nt(qb, jnp.concatenate([k_new, pad], axis=0).astype(BF16)) * DH_B ** -0.5
        dist = past + _iota(s_c.shape, 0) % t_len - _iota(s_c.shape, 1)
        s_c = jnp.where((dist % d == 0) & (dist // d <= SPAN), s_c, MASK_NEG)
        col = _iota(s_n.shape, 1)
        dist = _iota(s_n.shape, 0) % t_len - col
        s_n = jnp.where((col < t_len) & (dist >= 0) & (dist % d == 0) & (dist // d <= SPAN), s_n, MASK_NEG)
        m = jnp.maximum(jnp.max(s_c, -1, keepdims=True), jnp.max(s_n, -1, keepdims=True))
        e_c, e_n = jnp.exp(s_c - m), jnp.exp(s_n - m)
        den = jnp.sum(e_c, -1, keepdims=True) + jnp.sum(e_n, -1, keepdims=True)
        pv = _nt(e_c.astype(BF16), c_ref[1].astype(BF16))
        pv = pv + _dot(e_n.astype(BF16), jnp.concatenate([v_new, pad], axis=0).astype(BF16))
        outs.append(pv / den)
        lses.append(m + jnp.log(den))
    m = jnp.maximum(jnp.maximum(lses[0], lses[1]), lses[2])
    ws = [jnp.exp(l - m) for l in lses]
    merged = (ws[0] * outs[0] + ws[1] * outs[1] + ws[2] * outs[2]) / (ws[0] + ws[1] + ws[2])
    merged = jnp.where(own_head, merged, 0.0).reshape(H_B, t_len, W_B).sum(axis=0)
    o_ref[...] = (merged * _silu(p_ref[:, 13 * W_B:14 * W_B])).astype(BF16)


def step_attn(proj, caches, tables, e, bsz, t_len):
    views = [c.transpose(0, 1, 3, 4, 5, 2).reshape(c.shape[0], bsz, 2, W_B, c.shape[2]) for c in caches]
    specs = [pl.BlockSpec((None, None, 2, W_B, v.shape[4]), lambda b: (e, b, 0, 0, 0)) for v in views]
    tab = pl.BlockSpec((t_len, LANES), lambda b: (0, 0))
    rows = pl.BlockSpec((1, t_len, 2 * W_B), lambda b: (b, 0, 0))
    return pl.pallas_call(
        functools.partial(_step_attn_kernel, t_len=t_len),
        grid=(bsz,),
        in_specs=[pl.BlockSpec((t_len, EVEN_N), lambda b: (b, 0))] + specs + [tab, tab, tab],
        out_specs=[pl.BlockSpec((t_len, W_B), lambda b: (b, 0)), rows, rows, rows],
        out_shape=[jax.ShapeDtypeStruct((bsz * t_len, W_B), BF16)]
        + [jax.ShapeDtypeStruct((bsz, t_len, 2 * W_B), F32)] * 3,
        compiler_params=_params("parallel"),
        name="step_attn",
    )(proj, *views, *tables)


def _ssd_kernel(z_ref, x_ref, bc_ref, dt_ref, dtt_ref, cw_ref, cb_ref, dtb_ref, dtbc_ref, nega_ref, negac_ref,
                dskip_ref, ng_ref, expand_ref, conv0_ref, s0_ref,
                y_ref, convn_ref, sfin_ref, ext_ref, st_ref, acc_ref, *, chunk):
    c = pl.program_id(1)
    tail = CONV_W - 1

    @pl.when(c == 0)
    def _():
        ext_ref[0:8] = jnp.concatenate([jnp.zeros((8 - tail, CONV_DIM_C), F32), conv0_ref[0]], axis=0)
        st_ref[...] = s0_ref[0]

    ext_ref[8:8 + chunk, :D_INNER_C] = x_ref[...]
    ext_ref[8:8 + chunk, D_INNER_C:] = bc_ref[...]
    cw = cw_ref[...]
    conv = cb_ref[...] + sum(cw[j:j + 1, :] * ext_ref[8 - tail + j:8 - tail + j + chunk, :] for j in range(CONV_W))
    xbc = _silu(conv)

    @pl.when(c == pl.num_programs(1) - 1)
    def _():
        convn_ref[0] = ext_ref[8 + chunk - tail:8 + chunk, :]

    ext_ref[0:8] = ext_ref[chunk:chunk + 8]
    xs = xbc[:, :D_INNER_C]
    gn = G_C * N_C
    bm = xbc[:, D_INNER_C:D_INNER_C + gn].astype(BF16)
    cm = xbc[:, D_INNER_C + gn:].astype(BF16)

    softplus = lambda a: jnp.maximum(a, 0.0) + jnp.log1p(jnp.exp(-jnp.abs(a)))
    dt = softplus(dt_ref[...] + dtb_ref[...])
    la = dt * nega_ref[...]
    la_t = softplus(dtt_ref[0, 0] + dtbc_ref[...]) * negac_ref[...]
    ii = _iota((chunk, chunk), 0)
    jj = _iota((chunk, chunk), 1)
    causal = ii >= jj
    if chunk % 16 == 0:
        select = lambda sel, a, lhs: _dot_select(sel.astype(BF16), a, lhs)
    else:
        select = lambda sel, a, lhs: _dot_exact(sel.astype(F32), a) if lhs else _dot_exact(a, sel.astype(F32))
    b = select(causal, la, True)
    b_t = select(ii <= jj, la_t, False)
    expand = expand_ref[...]
    b_x = select(expand, b, False)
    dt_x = select(expand, dt, False)
    b_last = b_x[chunk - 1:chunk, :]
    xdt = xs * dt_x
    xdt_b = xdt.astype(BF16)
    xdtw = (xdt * jnp.exp(b_last - b_x)).astype(BF16)
    e_b = jnp.exp(b_x)
    low = _iota((chunk, LANES), 1) < P_C
    hg = H_C // G_C
    wg = hg * P_C
    g_heads = jnp.broadcast_to(jnp.exp(b_t[:, chunk - 1:chunk]), (H_C, N_C))
    row_decay = jnp.concatenate([jnp.broadcast_to(g_heads[h:h + 1, :], (P_C, N_C)) for h in range(H_C)], axis=0)
    for g in range(G_C):
        cg, bg = cm[:, g * N_C:(g + 1) * N_C], bm[:, g * N_C:(g + 1) * N_C]
        gs = slice(g * wg, (g + 1) * wg)
        acc_ref[:, gs] = _nt(cg, st_ref[gs, :].astype(BF16)) * e_b[:, gs]
        scores = _nt(cg, bg)
        for hp in range(hg // 2):
            sl = slice(g * wg + hp * LANES, g * wg + (hp + 1) * LANES)
            res = []
            for h in (g * hg + 2 * hp, g * hg + 2 * hp + 1):
                decay = jnp.exp(jnp.where(causal, b[:, h:h + 1] - b_t[h:h + 1, :], MASK_NEG))
                res.append(_dot((scores * decay).astype(BF16), xdt_b[:, sl]))
            acc_ref[:, sl] += jnp.where(low, res[0], res[1])
        st_ref[gs, :] = st_ref[gs, :] * row_decay[gs, :] + _tn(xdtw[:, gs], bg)

    y = (acc_ref[...] + dskip_ref[...] * xs) * _silu(z_ref[...])
    for g in range(G_C):
        gs = slice(g * wg, (g + 1) * wg)
        yg = y[:, gs]
        y_ref[:, gs] = (yg * lax.rsqrt(jnp.mean(yg * yg, -1, keepdims=True) + RMS_EPS) * ng_ref[:, gs]).astype(BF16)

    @pl.when(c == pl.num_programs(1) - 1)
    def _():
        sfin_ref[0] = st_ref[...]


def ssd(proj, dt_t, prm, conv0_all, s0, o, bsz, t_len, chunk):
    nc = t_len // chunk
    blk = lambda w, j: pl.BlockSpec((chunk, w), lambda b, c, j=j: (b * nc + c, j))
    const = lambda a: pl.BlockSpec(a.shape, lambda b, c: (0,) * a.ndim)
    consts = [prm["conv_w"], prm["conv_b"], prm["dt_bias"], prm["dt_bias_col"], prm["neg_a"], prm["neg_a_col"],
              prm["d_skip"], prm["norm_g"], prm["expand"]]
    return pl.pallas_call(
        functools.partial(_ssd_kernel, chunk=chunk),
        grid=(bsz, nc),
        in_specs=[blk(D_INNER_C, 0), blk(D_INNER_C, 1), blk(2 * G_C * N_C, 4), blk(LANES, ODD_MAIN // LANES),
                  pl.BlockSpec((1, 1, H_C, chunk), lambda b, c: (b, c, 0, 0))]
        + [const(a) for a in consts]
        + [pl.BlockSpec((None, 1, CONV_W - 1, CONV_DIM_C), lambda b, c: (o, b, 0, 0)),
           pl.BlockSpec((None, 1, D_INNER_C, N_C), lambda b, c: (o, b, 0, 0))],
        out_specs=[pl.BlockSpec((chunk, D_INNER_C), lambda b, c: (b * nc + c, 0)),
                   pl.BlockSpec((1, CONV_W - 1, CONV_DIM_C), lambda b, c: (b, 0, 0)),
                   pl.BlockSpec((1, D_INNER_C, N_C), lambda b, c: (b, 0, 0))],
        out_shape=[jax.ShapeDtypeStruct((bsz * t_len, D_INNER_C), BF16),
                   jax.ShapeDtypeStruct((bsz, CONV_W - 1, CONV_DIM_C), F32),
                   jax.ShapeDtypeStruct((bsz, D_INNER_C, N_C), F32)],
        scratch_shapes=[pltpu.VMEM((chunk + 8, CONV_DIM_C), F32), pltpu.VMEM((D_INNER_C, N_C), F32),
                        pltpu.VMEM((chunk, D_INNER_C), F32)],
        compiler_params=_params("parallel", "arbitrary"),
        name="ssd",
    )(proj, proj, proj, proj, dt_t, *consts, conv0_all, s0)


def _retention_kernel(q_ref, k_ref, v_ref, g_ref, cf_ref, sa_ref, sb_ref, dmat_ref, gq_ref, gw_ref, gs_ref, s0_ref,
                      o_ref, sfin_ref, st_ref, *, chunk):
    c = pl.program_id(1)

    nseq = s0_ref.shape[0]
    wst = H_D * DK_D

    @pl.when(c == 0)
    def _():
        for i in range(nseq):
            st_ref[i * wst:(i + 1) * wst, :] = s0_ref[i].reshape(wst, DV_D)

    half = DK_D // 2
    cf, sa, sb = cf_ref[...], sa_ref[...], sb_ref[...]
    gate = _silu(g_ref[...])
    low = _iota((chunk, LANES), 1) < DK_D
    items = [(i, h) for i in range(nseq) for h in range(H_D)]
    seq = lambda i: slice(i * chunk, (i + 1) * chunk)
    pair = lambda a, h: a[:, (h // 2) * LANES:(h // 2 + 1) * LANES]
    sel = lambda h: low if h % 2 == 0 else ~low
    qr = [_rope(q_ref[seq(i), :], cf, sa, sb, half) for i in range(nseq)]
    kr = [_rope(k_ref[seq(i), :], cf, sa, sb, half) * DK_D ** -0.5 for i in range(nseq)]
    vh = {(i, h): v_ref[seq(i), h * DV_D:(h + 1) * DV_D].astype(BF16) for i, h in items}
    states = {(i, hp): st_ref[i * wst + hp * LANES:i * wst + (hp + 1) * LANES, :]
              for i in range(nseq) for hp in range(H_D // 2)}
    qm = {(i, h): jnp.where(sel(h), pair(qr[i], h), 0.0).astype(BF16) for i, h in items}
    km = {(i, h): (jnp.where(sel(h), pair(kr[i], h), 0.0) * gw_ref[h]).astype(BF16) for i, h in items}
    scores = {(i, h): _nt(qm[i, h], pair(kr[i], h).astype(BF16)) for i, h in items}
    inter = {(i, h): _dot(qm[i, h], states[i, h // 2].astype(BF16)) for i, h in items}
    upd = {(i, h): _tn(km[i, h], vh[i, h]) for i, h in items}
    atts = {(i, h): (scores[i, h] * dmat_ref[h]).astype(BF16) for i, h in items}
    intra = {(i, h): _dot(atts[i, h], vh[i, h]) for i, h in items}
    for h in range(H_D):
        cols = []
        for i in range(nseq):
            oh = intra[i, h] + inter[i, h] * gq_ref[h]
            mu = jnp.mean(oh, -1, keepdims=True)
            cen = oh - mu
            var = jnp.mean(cen * cen, -1, keepdims=True)
            cols.append(cen * lax.rsqrt(var + LN_EPS))
        o_ref[:, h * DV_D:(h + 1) * DV_D] = (jnp.concatenate(cols, axis=0) * gate[:, h * DV_D:(h + 1) * DV_D]).astype(BF16)
    for (i, hp), s in states.items():
        st_ref[i * wst + hp * LANES:i * wst + (hp + 1) * LANES, :] = s * gs_ref[hp] + upd[i, 2 * hp] + upd[i, 2 * hp + 1]

    @pl.when(c == pl.num_programs(1) - 1)
    def _():
        for i in range(nseq):
            sfin_ref[i] = st_ref[i * wst:(i + 1) * wst, :].reshape(H_D, DK_D, DV_D)


def retention(proj, tables, consts, s0_all, o, bsz, t_len, chunk, nseq=1):
    nc = t_len // chunk
    assert nseq == 1 or nc == 1
    blk = lambda w, j: pl.BlockSpec((nseq * chunk, w), lambda b, c, j=j: (b * nc + c, j))
    tab = pl.BlockSpec((chunk, LANES), lambda b, c: (c, 0))
    const = lambda a: pl.BlockSpec(a.shape, lambda b, c: (0,) * a.ndim)
    wqk = H_D * DK_D
    return pl.pallas_call(
        functools.partial(_retention_kernel, chunk=chunk),
        grid=(bsz // nseq, nc),
        in_specs=[blk(wqk, 2560 // wqk), blk(wqk, 2816 // wqk), blk(W_D, 3072 // W_D), blk(W_D, 3584 // W_D),
                  tab, tab, tab] + [const(a) for a in consts]
        + [pl.BlockSpec((None, nseq, H_D, DK_D, DV_D), lambda b, c: (o, b, 0, 0, 0))],
        out_specs=[pl.BlockSpec((nseq * chunk, W_D), lambda b, c: (b * nc + c, 0)),
                   pl.BlockSpec((nseq, H_D, DK_D, DV_D), lambda b, c: (b, 0, 0, 0))],
        out_shape=[jax.ShapeDtypeStruct((bsz * t_len, W_D), BF16),
                   jax.ShapeDtypeStruct((bsz, H_D, DK_D, DV_D), F32)],
        scratch_shapes=[pltpu.VMEM((nseq * H_D * DK_D, DV_D), F32)],
        compiler_params=_params("parallel", "arbitrary"),
        name="retention",
    )(proj, proj, proj, proj, *tables, *consts, s0_all)


def _retention_consts(chunk):
    lg = np.log1p(-np.exp2(-5.0 - np.arange(H_D, dtype=np.float64)))
    t = np.arange(chunk)
    diff = t[:, None] - t[None, :]
    dmat = np.where(diff >= 0, np.exp(lg[:, None, None] * diff), 0.0)
    gq = np.broadcast_to(np.exp(lg[:, None, None] * (t[None, :, None] + 1)), (H_D, chunk, DV_D))
    gw = np.broadcast_to(np.exp(lg[:, None, None] * (chunk - 1 - t[None, :, None])), (H_D, chunk, LANES))
    gs = np.repeat(np.exp(lg * chunk), DK_D).reshape(H_D // 2, 2 * DK_D, 1)
    gs = np.broadcast_to(gs, (H_D // 2, 2 * DK_D, DV_D))
    return [jnp.asarray(a, F32) for a in (dmat, gq, gw, gs)]


MEM_LANE_TILES = DH_M // LANES
MEM_ROW_PERIOD = 2 * MEM_LANE_TILES * H_M


def mem_kv_rows(kv):
    lead = kv.shape[:-4]
    kv = kv.reshape(*lead, N_MEM, 2, H_M, MEM_LANE_TILES, LANES)
    return jnp.swapaxes(kv, -3, -2).reshape(*lead, N_MEM * MEM_ROW_PERIOD, LANES)


def _mem_attn_kernel(q_ref, kv_ref, o_ref):
    nseq = kv_ref.shape[0]
    rows = q_ref.shape[0] // nseq

    def head_rows(i, which, h):
        tiles = [kv_ref[i, pl.ds((which * MEM_LANE_TILES + j) * H_M + h, N_MEM, stride=MEM_ROW_PERIOD), :]
                 for j in range(MEM_LANE_TILES)]
        return jnp.concatenate(tiles, axis=1).astype(BF16)

    items = [(i, h) for i in range(nseq) for h in range(H_M)]
    block = lambda i, h: (slice(i * rows, (i + 1) * rows), slice(h * DH_M, (h + 1) * DH_M))
    scores = [_nt(q_ref[block(i, h)].astype(BF16), head_rows(i, 0, h)) * DH_M ** -0.5 for i, h in items]
    probs, dens = [], []
    for s in scores:
        p = jnp.exp(s - jnp.max(s, -1, keepdims=True))
        dens.append(jnp.sum(p, -1, keepdims=True))
        probs.append(p.astype(BF16))
    outs = [_dot(p, head_rows(i, 1, h)) for (i, h), p in zip(items, probs)]
    normed = {item: o / den for item, o, den in zip(items, outs, dens)}
    for h in range(H_M):
        col = jnp.concatenate([normed[(i, h)] for i in range(nseq)], axis=0)
        o_ref[:, h * DH_M:(h + 1) * DH_M] = col.astype(BF16)


MEM_SEQS_PER_STEP = 4


def mem_attn(q, kv, layer, bsz, t_len):
    tq = min(t_len, 512)
    nq = t_len // tq
    nseq = MEM_SEQS_PER_STEP if (nq == 1 and tq * MEM_SEQS_PER_STEP <= 512 and bsz % MEM_SEQS_PER_STEP == 0) else 1
    tq *= nseq
    return pl.pallas_call(
        _mem_attn_kernel,
        grid=(bsz // nseq, nq),
        in_specs=[pl.BlockSpec((tq, D_MODEL), lambda b, i: (b * nq + i, 0)),
                  pl.BlockSpec((None, nseq, N_MEM * MEM_ROW_PERIOD, LANES), lambda b, i: (layer, b, 0, 0))],
        out_specs=pl.BlockSpec((tq, D_MODEL), lambda b, i: (b * nq + i, 0)),
        out_shape=jax.ShapeDtypeStruct((bsz * t_len, D_MODEL), BF16),
        compiler_params=_params("parallel", "arbitrary"),
        name="mem_attn",
    )(q, kv)


def _even_layer(x, xb, e, w_in, w_out, lb, gn, ln_g, ln_b, state_a, bsz, t_len, tables, caches):
    proj = matmul(xb, w_in, 512)
    if caches is None:
        o_a, s_a = hgrn2(proj, lb, gn, state_a, 0, bsz, t_len, GLA_CHUNK, GLA_SUB, GLA_ROWS)
        res = [band_attn(proj, tables, g, bsz, t_len) for g in range(N_BG)]
        o_b = merge_groups([r[0] for r in res], [r[1] for r in res], proj)
        rows = [window_rows(proj, tables, g, bsz, t_len) for g in range(N_BG)]
    else:
        o_a, s_a = hgrn2(proj, lb, gn, state_a, e, bsz, t_len, t_len, t_len, t_len, STEP_SEQS if bsz % STEP_SEQS == 0 else 1)
        o_b, *rows = step_attn(proj, caches, tables, e, bsz, t_len)
    x, xb = outproj_ln([o_a, o_b], [w_out[:W_A], w_out[W_A:]], x, ln_g, ln_b)
    rows = [r.reshape(bsz, r.shape[1], 2, H_B, DH_B) for r in rows]
    return x, xb, s_a, rows


def _odd_layer(x, xb, o, w_in, w_out, prm, ln_g, ln_b, s_c0, conv0, s_d0, bsz, t_len, chunk, tables, ret_consts):
    proj = matmul(xb, w_in, 384)
    nc = t_len // chunk
    dt_t = proj[:, ODD_MAIN:ODD_MAIN + H_C].reshape(bsz, nc, chunk, H_C).transpose(0, 1, 3, 2)
    y_c, conv_n, s_c = ssd(proj, dt_t, prm, conv0, s_c0, o, bsz, t_len, chunk)
    nseq = STEP_SEQS if (nc == 1 and bsz % STEP_SEQS == 0) else 1
    o_d, s_d = retention(proj, tables, ret_consts, s_d0, o, bsz, t_len, chunk, nseq)
    x, xb = outproj_ln([y_c, o_d], [w_out[:D_INNER_C], w_out[D_INNER_C:]], x, ln_g, ln_b)
    return x, xb, s_c, conv_n, s_d


def _memory_layer(x, xb, layer, wq, wo, kv, ln_g, ln_b, bsz, t_len):
    q = matmul(xb, wq, 512)
    att = mem_attn(q, kv, layer, bsz, t_len)
    return outproj_ln([att], [wo], x, ln_g, ln_b)


def _state_to_rows(s):
    return jnp.swapaxes(s, -1, -2).reshape(*s.shape[:-3], D_INNER_C, N_C)


def _state_from_rows(s):
    return jnp.swapaxes(s.reshape(*s.shape[:-2], H_C, P_C, N_C), -1, -2)


def kernel(x_prompt, x_sample, state_a, cache_b1, cache_b2, cache_b3, state_c_ssm, state_c_conv, state_d,
           cache_mem_kv, mem_prompt, w_in_even, a_lb_logits, a_norm_g, w_out_even, w_in_odd, c_conv_w, c_conv_b,
           c_dt_bias, c_a_log, c_d_skip, c_norm_g, w_out_odd, ln1_g, ln1_b, ln2_g, ln2_b, m_wq, m_wkv, m_wo):
    bp, tp = x_prompt.shape[0], x_prompt.shape[1]
    bs, ts = x_sample.shape[0], x_sample.shape[1]
    pos_p = jnp.arange(tp, dtype=jnp.int32)
    pos_s = PAST_LEN + jnp.arange(ts, dtype=jnp.int32)
    tab_b = {"p": _rope_tables(pos_p, ROT_DIM_B, DH_B, ROPE_THETA), "s": _rope_tables(pos_s, ROT_DIM_B, DH_B, ROPE_THETA)}
    tab_d = {"p": _rope_tables(pos_p, DK_D, DK_D, RET_THETA), "s": _rope_tables(pos_s, DK_D, DK_D, RET_THETA)}
    ret_consts = {"p": _retention_consts(RET_CHUNK), "s": _retention_consts(ts)}

    sm = jax.nn.softmax(a_lb_logits.astype(F32), axis=0)
    lb_all = jnp.cumsum(sm, axis=0) - sm[0]
    expand = jnp.asarray(np.pad(np.repeat(np.eye(H_C), P_C, axis=1), ((0, LANES - H_C), (0, 0))), F32)
    pad_h = lambda a: jnp.pad(a.astype(F32), (0, LANES - H_C))[None, :]
    dt0 = D_INNER_C + CONV_DIM_C
    zeros_a = jnp.zeros((1, bp, H_A, DK_A, DV_A), F32)
    zeros_c = jnp.zeros((1, bp, D_INNER_C, N_C), F32)
    ssm_rows = _state_to_rows(state_c_ssm)
    zeros_conv = jnp.zeros((1, bp, CONV_W - 1, CONV_DIM_C), F32)
    zeros_d = jnp.zeros((1, bp, H_D, DK_D, DV_D), F32)
    caches = (cache_b1, cache_b2, cache_b3)
    mem_kv_s = mem_kv_rows(cache_mem_kv)

    xp, xs = x_prompt.reshape(bp * tp, D_MODEL), x_sample.reshape(bs * ts, D_MODEL)
    xpb, xsb = xp.astype(BF16), xs.astype(BF16)
    mem_b = mem_prompt.reshape(bp * N_MEM, D_MODEL).astype(BF16)
    a_p, a_s, c_p, c_s, cv_p, cv_s, d_p, d_s, mem_p = [], [], [], [], [], [], [], [], []
    b_p = [[] for _ in range(N_BG)]
    b_s = [[] for _ in range(N_BG)]
    for l in range(DEPTH):
        if l % 2 == 0:
            e = l // 2
            w_in, w_out = w_in_even[e].astype(BF16), w_out_even[e].astype(BF16)
            lb = lb_all[e][None, :]
            gn = jnp.tile(a_norm_g[e].astype(F32), H_A)[None, :]
            g1, b1 = ln1_g[l][None, :], ln1_b[l][None, :]
            xp, xpb, sap, rp = _even_layer(xp, xpb, e, w_in, w_out, lb, gn, g1, b1, zeros_a, bp, tp, tab_b["p"], None)
            xs, xsb, sas, rs = _even_layer(xs, xsb, e, w_in, w_out, lb, gn, g1, b1, state_a, bs, ts, tab_b["s"], caches)
            a_p.append(sap)
            a_s.append(sas)
            for g in range(N_BG):
                b_p[g].append(rp[g])
                b_s[g].append(rs[g])
        else:
            o = l // 2
            w = w_in_odd[o]
            w_in = jnp.concatenate([w[:, :dt0], w[:, dt0 + H_C:], w[:, dt0:dt0 + H_C],
                                    jnp.zeros((D_MODEL, LANES - H_C), w.dtype)], axis=1).astype(BF16)
            w_out = w_out_odd[o].astype(BF16)
            prm = {
                "conv_w": c_conv_w[o].astype(F32), "conv_b": c_conv_b[o].astype(F32)[None, :],
                "dt_bias": pad_h(c_dt_bias[o]), "dt_bias_col": c_dt_bias[o].astype(F32)[:, None],
                "neg_a": pad_h(-jnp.exp(c_a_log[o].astype(F32))), "neg_a_col": -jnp.exp(c_a_log[o].astype(F32))[:, None],
                "d_skip": jnp.repeat(c_d_skip[o].astype(F32), P_C)[None, :], "norm_g": c_norm_g[o].astype(F32)[None, :],
                "expand": expand,
            }
            g1, b1 = ln1_g[l][None, :], ln1_b[l][None, :]
            xp, xpb, scp, cvp, sdp = _odd_layer(xp, xpb, 0, w_in, w_out, prm, g1, b1, zeros_c, zeros_conv, zeros_d,
                                                bp, tp, SSD_CHUNK, tab_d["p"], ret_consts["p"])
            xs, xsb, scs, cvs, sds = _odd_layer(xs, xsb, o, w_in, w_out, prm, g1, b1, ssm_rows,
                                                state_c_conv, state_d, bs, ts, ts, tab_d["s"], ret_consts["s"])
            c_p.append(_state_from_rows(scp))
            c_s.append(_state_from_rows(scs))
            cv_p.append(cvp)
            cv_s.append(cvs)
            d_p.append(sdp)
            d_s.append(sds)
        kv_p = matmul(mem_b, m_wkv[l].astype(BF16), 512)
        kv_p = kv_p.reshape(bp, N_MEM, 2, H_M, DH_M)
        mem_p.append(kv_p)
        wq, wo = m_wq[l].astype(BF16), m_wo[l].astype(BF16)
        g2, b2 = ln2_g[l][None, :], ln2_b[l][None, :]
        xp, xpb = _memory_layer(xp, xpb, 0, wq, wo, mem_kv_rows(kv_p)[None], g2, b2, bp, tp)
        xs, xsb = _memory_layer(xs, xsb, l, wq, wo, mem_kv_s, g2, b2, bs, ts)
    return (xp.reshape(bp, tp, D_MODEL), xs.reshape(bs, ts, D_MODEL), jnp.stack(a_p), jnp.stack(a_s),
            jnp.stack(b_p[0]), jnp.stack(b_p[1]), jnp.stack(b_p[2]),
            jnp.stack(b_s[0]), jnp.stack(b_s[1]), jnp.stack(b_s[2]),
            jnp.stack(c_p), jnp.stack(c_s), jnp.stack(cv_p), jnp.stack(cv_s),
            jnp.stack(d_p), jnp.stack(d_s), jnp.stack(mem_p))
```

```python
import functools
import math

import numpy as np
import jax
import jax.numpy as jnp
from jax import lax
from jax.experimental import pallas as pl
from jax.experimental.pallas import tpu as pltpu

F32 = jnp.float32
BF16 = jnp.bfloat16
HIGHEST = lax.Precision.HIGHEST

D_MODEL = 1024
DEPTH = 4
PAST_LEN = 2048
H_A, DK_A, DV_A = 4, 128, 128
W_A = H_A * DV_A
GLA_CHUNK = 64
GLA_SUB = 16
GLA_ROWS = 512
STEP_SEQS = 4
B_WINDOWS = (128, 512, 2048)
B_DILATIONS = (1, 4, 16)
N_BG, H_B, DH_B = 3, 8, 64
W_B = H_B * DH_B
SPAN = 128
ROT_DIM_B = DH_B // 4
ROPE_THETA = 500000.0
H_C, P_C, G_C, N_C, CONV_W = 16, 64, 2, 128, 4
D_INNER_C = H_C * P_C
CONV_DIM_C = D_INNER_C + 2 * G_C * N_C
SSD_CHUNK = 256
H_D, DK_D, DV_D = 4, 64, 128
W_D = H_D * DV_D
RET_CHUNK = 256
RET_THETA = 10000.0
H_M = 4
DH_M = D_MODEL // H_M
N_MEM = 256

ALPHA = (2.0 * DEPTH) ** 0.25
LN_EPS = 1e-5
RMS_EPS = 1e-6
MASK_NEG = -1e30

LANES = 128
EVEN_N = 7168
ODD_MAIN = 4096
ODD_N = ODD_MAIN + LANES
VMEM_LIMIT = 56 * 1024 * 1024


def _params(*sem):
    return pltpu.CompilerParams(dimension_semantics=sem, vmem_limit_bytes=VMEM_LIMIT)


def _nt(a, b):
    return lax.dot_general(a, b, (((1,), (1,)), ((), ())), preferred_element_type=F32)


def _tn(a, b):
    return lax.dot_general(a, b, (((0,), (0,)), ((), ())), preferred_element_type=F32)


def _dot(a, b):
    return jnp.dot(a, b, preferred_element_type=F32)


def _dot_exact(a, b):
    return jnp.dot(a, b, preferred_element_type=F32, precision=HIGHEST)


def _dot_select(sel, x, sel_is_lhs):
    hi = x.astype(BF16)
    rest = x - hi.astype(F32)
    mid = rest.astype(BF16)
    lo = (rest - mid.astype(F32)).astype(BF16)
    mm = (lambda t: _dot(sel, t)) if sel_is_lhs else (lambda t: _dot(t, sel))
    return mm(hi) + mm(mid) + mm(lo)


def _silu(x):
    return x * jax.nn.sigmoid(x)


def _iota(shape, dim):
    return lax.broadcasted_iota(jnp.int32, shape, dim)


def _rope(x, cf, sa, sb, shift):
    outs = []
    for j in range(x.shape[1] // LANES):
        xj = x[:, j * LANES:(j + 1) * LANES]
        outs.append(xj * cf + pltpu.roll(xj, shift, 1) * sa + pltpu.roll(xj, LANES - shift, 1) * sb)
    return outs[0] if len(outs) == 1 else jnp.concatenate(outs, axis=1)


def _rope_tables(pos, rot_dim, head_dim, theta):
    half = rot_dim // 2
    inv_freq = theta ** (-jnp.arange(half, dtype=F32) / half)
    ang = pos.astype(F32)[:, None] * inv_freq
    cos, sin = jnp.cos(ang), jnp.sin(ang)
    lane = np.arange(LANES) % head_dim
    idx = lane % half
    cf = jnp.where(lane < rot_dim, cos[:, idx], 1.0)
    sa = jnp.where((lane >= half) & (lane < rot_dim), sin[:, idx], 0.0)
    sb = jnp.where(lane < half, -sin[:, idx], 0.0)
    return cf.astype(F32), sa.astype(F32), sb.astype(F32)


def _mm_kernel(x_ref, w_ref, o_ref):
    o_ref[...] = _dot(x_ref[...], w_ref[...])


def matmul(x, w, tn):
    m, k = x.shape
    n = w.shape[1]
    tm = min(m, 2048)
    return pl.pallas_call(
        _mm_kernel,
        grid=(m // tm, n // tn),
        in_specs=[pl.BlockSpec((tm, k), lambda i, j: (i, 0)), pl.BlockSpec((k, tn), lambda i, j: (0, j))],
        out_specs=pl.BlockSpec((tm, tn), lambda i, j: (i, j)),
        out_shape=jax.ShapeDtypeStruct((m, n), F32),
        compiler_params=_params("parallel", "parallel"),
        name="matmul",
    )(x, w)


def _outproj_ln_kernel(*refs, n_parts):
    parts, ws = refs[:n_parts], refs[n_parts:2 * n_parts]
    x_ref, g_ref, b_ref, o_ref, ob_ref = refs[2 * n_parts:]
    acc = ALPHA * x_ref[...]
    for p_ref, w_ref in zip(parts, ws):
        acc = acc + _dot(p_ref[...], w_ref[...])
    mu = jnp.mean(acc, -1, keepdims=True)
    cen = acc - mu
    var = jnp.mean(cen * cen, -1, keepdims=True)
    y = cen * lax.rsqrt(var + LN_EPS) * g_ref[...] + b_ref[...]
    o_ref[...] = y
    ob_ref[...] = y.astype(BF16)


def outproj_ln(parts, ws, x, g, b):
    m = x.shape[0]
    tm = min(m, 512)
    row = lambda w: pl.BlockSpec((tm, w), lambda i: (i, 0))
    const = lambda a: pl.BlockSpec(a.shape, lambda i: (0, 0))
    return pl.pallas_call(
        functools.partial(_outproj_ln_kernel, n_parts=len(parts)),
        grid=(m // tm,),
        in_specs=[row(p.shape[1]) for p in parts] + [const(w) for w in ws] + [row(D_MODEL), const(g), const(b)],
        out_specs=[row(D_MODEL), row(D_MODEL)],
        out_shape=[jax.ShapeDtypeStruct((m, D_MODEL), F32), jax.ShapeDtypeStruct((m, D_MODEL), BF16)],
        compiler_params=_params("parallel"),
        name="outproj_ln",
    )(*parts, *ws, x, g, b)


def _hgrn2_kernel(q_ref, f_ref, v_ref, g_ref, lb_ref, gn_ref, s0_ref, o_ref, sfin_ref, st_ref, *, chunk, sub):
    c = pl.program_id(1)
    nseq = s0_ref.shape[0]
    seq_rows = q_ref.shape[0] // nseq

    @pl.when(c == 0)
    def _():
        for i in range(nseq):
            for h in range(H_A):
                st_ref[i * H_A + h] = s0_ref[i, h].T

    lb = lb_ref[...]
    gn = gn_ref[...]
    starts = range(0, q_ref.shape[0], chunk)
    nsub = chunk // sub
    tril = _iota((chunk, chunk), 0) >= _iota((chunk, chunk), 1)
    if chunk % 16 == 0:
        cumsum = lambda a: _dot_select(tril.astype(BF16), a, True)
    else:
        cumsum = lambda a: _dot_exact(tril.astype(F32), a)
    head = lambda a, h: a[:, h * DK_A:(h + 1) * DK_A]
    lane_head = _iota((H_A * sub, W_A), 1) // DK_A == _iota((H_A * sub, W_A), 0) // sub

    fx = f_ref[...]
    log_f = jnp.log(lb + (1.0 - lb) * jax.nn.sigmoid(fx))
    k = (1.0 - lb) * jax.nn.sigmoid(-fx)
    q = _silu(q_ref[...])
    v = v_ref[...].astype(BF16)
    bs = [cumsum(log_f[c0:c0 + chunk]) for c0 in starts]

    q_in, g_last, kw, q_bd, kt = [], [], [], [], []
    for c0, b in zip(starts, bs):
        qc, kc = q[c0:c0 + chunk], k[c0:c0 + chunk]
        b_last = b[chunk - 1:chunk, :]
        q_in.append((qc * jnp.exp(b)).astype(BF16))
        g_last.append(jnp.exp(b_last))
        kw.append((kc * jnp.exp(b_last - b)).astype(BF16))
        for i in range(nsub):
            r0, r1 = i * sub, (i + 1) * sub
            ref = b[r0 - 1:r0, :] if i > 0 else jnp.zeros((1, W_A), F32)
            qt = qc[r0:r1] * jnp.exp(b[r0:r1] - ref)
            q_bd.append(jnp.where(lane_head, jnp.concatenate([qt] * H_A, axis=0), 0.0).astype(BF16))
            kt.append((kc[:r1] * jnp.exp(ref - b[:r1])).astype(BF16))
    upd = [[_tn(v[c0:c0 + chunk, h * DV_A:(h + 1) * DV_A], head(kw_c, h)) for h in range(H_A)]
           for c0, kw_c in zip(starts, kw)]
    scores = [_nt(a, b_) for a, b_ in zip(q_bd, kt)]
    atts = []
    for idx, s in enumerate(scores):
        r0 = (idx % nsub) * sub
        causal = _iota(s.shape, 1) <= _iota(s.shape, 0) % sub + r0
        atts.append(jnp.where(causal, s, 0.0).astype(BF16))
    intra = []
    for idx, att in enumerate(atts):
        c0 = starts[idx // nsub]
        pv = _dot(att, v[c0:c0 + att.shape[1]])
        intra.append([pv[h * sub:(h + 1) * sub, h * DV_A:(h + 1) * DV_A] for h in range(H_A)])

    gate = _silu(g_ref[...])
    state = [st_ref[j] for j in range(nseq * H_A)]
    outs = [[] for _ in range(H_A)]
    for ci, c0 in enumerate(starts):
        for h in range(H_A):
            j = (c0 // seq_rows) * H_A + h
            oh = _nt(head(q_in[ci], h), state[j].astype(BF16))
            oh = oh + jnp.concatenate([intra[ci * nsub + i][h] for i in range(nsub)], axis=0)
            outs[h].append(oh * lax.rsqrt(jnp.mean(oh * oh, -1, keepdims=True) + RMS_EPS) * head(gn, h))
            state[j] = state[j] * head(g_last[ci], h) + upd[ci][h]
    for h in range(H_A):
        o_ref[:, h * DV_A:(h + 1) * DV_A] = (jnp.concatenate(outs[h], axis=0) * head(gate, h)).astype(BF16)
    for j in range(nseq * H_A):
        st_ref[j] = state[j]

    @pl.when(c == pl.num_programs(1) - 1)
    def _():
        for i in range(nseq):
            for h in range(H_A):
                sfin_ref[i, h] = st_ref[i * H_A + h].T


def hgrn2(proj, lb, gn, s0_all, e, bsz, t_len, chunk, sub, rows, nseq=1):
    nc = t_len // rows
    assert nseq == 1 or nc == 1
    col = lambda j: pl.BlockSpec((nseq * rows, W_A), lambda b, c, j=j: (b * nc + c, j))
    const = pl.BlockSpec((1, W_A), lambda b, c: (0, 0))
    state = pl.BlockSpec((None, nseq, H_A, DK_A, DV_A), lambda b, c: (e, b, 0, 0, 0))
    return pl.pallas_call(
        functools.partial(_hgrn2_kernel, chunk=chunk, sub=sub),
        grid=(bsz // nseq, nc),
        in_specs=[col(0), col(1), col(2), col(3), const, const, state],
        out_specs=[pl.BlockSpec((nseq * rows, W_A), lambda b, c: (b * nc + c, 0)),
                   pl.BlockSpec((nseq, H_A, DK_A, DV_A), lambda b, c: (b, 0, 0, 0))],
        out_shape=[jax.ShapeDtypeStruct((bsz * t_len, W_A), BF16),
                   jax.ShapeDtypeStruct((bsz, H_A, DK_A, DV_A), F32)],
        scratch_shapes=[pltpu.VMEM((nseq * H_A, DV_A, DK_A), F32)],
        compiler_params=_params("parallel", "arbitrary"),
        name="hgrn2",
    )(proj, proj, proj, proj, lb, gn, s0_all)


BAND_TILE = SPAN * max(B_DILATIONS)


BAND_UNROLL = 4


def _band_attn_kernel(q_ref, k_ref, v_ref, cf_ref, sa_ref, sb_ref, o_ref, l_ref, qs_ref, ks_ref, vs_ref, *, d):
    n = pl.program_id(2)
    half = ROT_DIM_B // 2
    reach = SPAN * d
    nblk = BAND_TILE // reach

    @pl.when(n == 0)
    def _():
        ks_ref[:BAND_TILE] = jnp.zeros((BAND_TILE, LANES), F32)
        vs_ref[:BAND_TILE] = jnp.zeros((BAND_TILE, LANES), F32)

    cf, sa, sb = cf_ref[...], sa_ref[...], sb_ref[...]
    qs_ref[...] = _rope(q_ref[...], cf, sa, sb, half)
    ks_ref[BAND_TILE:] = _rope(k_ref[...], cf, sa, sb, half)
    vs_ref[BAND_TILE:] = v_ref[...]

    i = _iota((SPAN, 2 * SPAN), 0)
    j = _iota((SPAN, 2 * SPAN), 1)
    dist = i + SPAN - j
    band = (dist >= 0) & (dist <= SPAN)
    low = _iota((SPAN, LANES), 1) < DH_B
    strided = lambda start: pl.ds(start, SPAN, stride=d) if d > 1 else pl.ds(start, SPAN)
    for grp in range(0, d * nblk, BAND_UNROLL):
        blocks = [(idx // nblk, idx % nblk) for idx in range(grp, grp + BAND_UNROLL)]
        starts = [jb * reach + r for r, jb in blocks]
        qm, kk, vv = [], [], []
        for st in starts:
            q = qs_ref[strided(st), :]
            qm.append([jnp.where(sel, q, 0.0).astype(BF16) for sel in (low, ~low)])
            cur, prev = strided(BAND_TILE + st), strided(BAND_TILE + st - reach)
            kk.append(jnp.concatenate([ks_ref[prev, :], ks_ref[cur, :]], axis=0).astype(BF16))
            vv.append(jnp.concatenate([vs_ref[prev, :], vs_ref[cur, :]], axis=0).astype(BF16))
        scores = [[_nt(qh, kk_u) * DH_B ** -0.5 for qh in qm_u] for qm_u, kk_u in zip(qm, kk)]
        probs, dens, lses = [], [], []
        for (r, jb), s_u in zip(blocks, scores):
            valid = band & ((j >= SPAN) | (n > 0) | (jb > 0))
            p_u, d_u, l_u = [], [], []
            for s in s_u:
                s = jnp.where(valid, s, MASK_NEG)
                m = jnp.max(s, -1, keepdims=True)
                p = jnp.exp(s - m)
                den = jnp.sum(p, -1, keepdims=True)
                p_u.append(p.astype(BF16))
                d_u.append(den)
                l_u.append(m + jnp.log(den))
            probs.append(p_u)
            dens.append(d_u)
            lses.append(l_u)
        pvs = [[_dot(p, vv_u) for p in p_u] for p_u, vv_u in zip(probs, vv)]
        for st, pv_u, d_u, l_u in zip(starts, pvs, dens, lses):
            o_ref[strided(st), :] = jnp.where(low, pv_u[0] / d_u[0], pv_u[1] / d_u[1])
            l_ref[strided(st), :] = jnp.where(low, l_u[0], l_u[1])

    ks_ref[BAND_TILE - reach:BAND_TILE] = ks_ref[2 * BAND_TILE - reach:]
    vs_ref[BAND_TILE - reach:BAND_TILE] = vs_ref[2 * BAND_TILE - reach:]


def band_attn(proj, tables, g, bsz, t_len):
    d = B_DILATIONS[g]
    nt = t_len // BAND_TILE
    npair = W_B // LANES
    col = lambda j: pl.BlockSpec((BAND_TILE, LANES), lambda b, hp, n, j=j: (b * nt + n, j * npair + hp))
    tab = pl.BlockSpec((BAND_TILE, LANES), lambda b, hp, n: (n, 0))
    out = pl.BlockSpec((BAND_TILE, LANES), lambda b, hp, n: (b * nt + n, hp))
    return pl.pallas_call(
        functools.partial(_band_attn_kernel, d=d),
        grid=(bsz, npair, nt),
        in_specs=[col(4 + g), col(7 + g), col(10 + g), tab, tab, tab],
        out_specs=[out, out],
        out_shape=[jax.ShapeDtypeStruct((bsz * t_len, W_B), F32)] * 2,
        scratch_shapes=[pltpu.VMEM((BAND_TILE, LANES), F32), pltpu.VMEM((2 * BAND_TILE, LANES), F32),
                        pltpu.VMEM((2 * BAND_TILE, LANES), F32)],
        compiler_params=_params("parallel", "parallel", "arbitrary"),
        name=f"band_attn_d{d}",
    )(proj, proj, proj, *tables)


def _merge_kernel(o1, o2, o3, l1, l2, l3, g_ref, o_ref):
    la, lb_, lc = l1[...], l2[...], l3[...]
    m = jnp.maximum(jnp.maximum(la, lb_), lc)
    wa, wb, wc = jnp.exp(la - m), jnp.exp(lb_ - m), jnp.exp(lc - m)
    merged = (wa * o1[...] + wb * o2[...] + wc * o3[...]) / (wa + wb + wc)
    o_ref[...] = (merged * _silu(g_ref[...])).astype(BF16)


def merge_groups(outs, lses, proj):
    m = proj.shape[0]
    tm = 512
    row = pl.BlockSpec((tm, W_B), lambda i: (i, 0))
    return pl.pallas_call(
        _merge_kernel,
        grid=(m // tm,),
        in_specs=[row] * 6 + [pl.BlockSpec((tm, W_B), lambda i: (i, EVEN_N // W_B - 1))],
        out_specs=row,
        out_shape=jax.ShapeDtypeStruct((m, W_B), BF16),
        compiler_params=_params("parallel"),
        name="merge_groups",
    )(*outs, *lses, proj)


def _window_rows_kernel(k_ref, v_ref, cf_ref, sa_ref, sb_ref, o_ref):
    o_ref[0, :, :W_B] = _rope(k_ref[...], cf_ref[...], sa_ref[...], sb_ref[...], ROT_DIM_B // 2)
    o_ref[0, :, W_B:] = v_ref[...]


def window_rows(proj, tables, g, bsz, t_len):
    keep = min(B_WINDOWS[g], t_len)
    nk, nt = keep // SPAN, t_len // SPAN
    col = lambda j: pl.BlockSpec((SPAN, W_B), lambda b, i, j=j: (b * nt + nt - nk + i, j))
    tab = pl.BlockSpec((SPAN, LANES), lambda b, i: (nt - nk + i, 0))
    return pl.pallas_call(
        _window_rows_kernel,
        grid=(bsz, nk),
        in_specs=[col(7 + g), col(10 + g), tab, tab, tab],
        out_specs=pl.BlockSpec((1, SPAN, 2 * W_B), lambda b, i: (b, i, 0)),
        out_shape=jax.ShapeDtypeStruct((bsz, keep, 2 * W_B), F32),
        compiler_params=_params("parallel", "parallel"),
        name="window_rows",
    )(proj, proj, *tables)


def _step_attn_kernel(p_ref, c1_ref, c2_ref, c3_ref, cf_ref, sa_ref, sb_ref, o_ref, r1_ref, r2_ref, r3_ref, *, t_len):
    half = ROT_DIM_B // 2
    cf, sa, sb = cf_ref[...], sa_ref[...], sb_ref[...]
    nrow = H_B * t_len
    own_head = (_iota((nrow, W_B), 1) // DH_B) == (_iota((nrow, W_B), 0) // t_len)
    pad = jnp.zeros((LANES - t_len, W_B), F32)
    outs, lses = [], []
    for g, (c_ref, r_ref) in enumerate(((c1_ref, r1_ref), (c2_ref, r2_ref), (c3_ref, r3_ref))):
        d = B_DILATIONS[g]
        past = c_ref.shape[2]
        q = _rope(p_ref[:, (4 + g) * W_B:(5 + g) * W_B], cf, sa, sb, half)
        k_new = _rope(p_ref[:, (7 + g) * W_B:(8 + g) * W_B], cf, sa, sb, half)
        v_new = p_ref[:, (10 + g) * W_B:(11 + g) * W_B]
        r_ref[0, :, :W_B] = k_new
        r_ref[0, :, W_B:] = v_new
        qb = jnp.where(own_head, jnp.concatenate([q] * H_B, axis=0), 0.0).astype(BF16)
        s_c = _dot(qb, c_ref[0].astype(BF16)) * DH_B ** -0.5
        s_n = _nt(qb, jnp.concatenate([k_new, pad], axis=0).astype(BF16)) * DH_B ** -0.5
        dist = past + _iota(s_c.shape, 0) % t_len - _iota(s_c.shape, 1)
        s_c = jnp.where((dist % d == 0) & (dist // d <= SPAN), s_c, MASK_NEG)
        col = _iota(s_n.shape, 1)
        dist = _iota(s_n.shape, 0) % t_len - col
        s_n = jnp.where((col < t_len) & (dist >= 0) & (dist % d == 0) & (dist // d <= SPAN), s_n, MASK_NEG)
        m = jnp.maximum(jnp.max(s_c, -1, keepdims=True), jnp.max(s_n, -1, keepdims=True))
        e_c, e_n = jnp.exp(s_c - m), jnp.exp(s_n - m)
        den = jnp.sum(e_c, -1, keepdims=True) + jnp.sum(e_n, -1, keepdims=True)
        pv = _nt(e_c.astype(BF16), c_ref[1].astype(BF16))
        pv = pv + _dot(e_n.astype(BF16), jnp.concatenate([v_new, pad], axis=0).astype(BF16))
        outs.append(pv / den)
        lses.append(m + jnp.log(den))
    m = jnp.maximum(jnp.maximum(lses[0], lses[1]), lses[2])
    ws = [jnp.exp(l - m) for l in lses]
    merged = (ws[0] * outs[0] + ws[1] * outs[1] + ws[2] * outs[2]) / (ws[0] + ws[1] + ws[2])
    merged = jnp.where(own_head, merged, 0.0).reshape(H_B, t_len, W_B).sum(axis=0)
    o_ref[...] = (merged * _silu(p_ref[:, 13 * W_B:14 * W_B])).astype(BF16)


def step_attn(proj, caches, tables, e, bsz, t_len):
    views = [c.transpose(0, 1, 3, 4, 5, 2).reshape(c.shape[0], bsz, 2, W_B, c.shape[2]) for c in caches]
    specs = [pl.BlockSpec((None, None, 2, W_B, v.shape[4]), lambda b: (e, b, 0, 0, 0)) for v in views]
    tab = pl.BlockSpec((t_len, LANES), lambda b: (0, 0))
    rows = pl.BlockSpec((1, t_len, 2 * W_B), lambda b: (b, 0, 0))
    return pl.pallas_call(
        functools.partial(_step_attn_kernel, t_len=t_len),
        grid=(bsz,),
        in_specs=[pl.BlockSpec((t_len, EVEN_N), lambda b: (b, 0))] + specs + [tab, tab, tab],
        out_specs=[pl.BlockSpec((t_len, W_B), lambda b: (b, 0)), rows, rows, rows],
        out_shape=[jax.ShapeDtypeStruct((bsz * t_len, W_B), BF16)]
        + [jax.ShapeDtypeStruct((bsz, t_len, 2 * W_B), F32)] * 3,
        compiler_params=_params("parallel"),
        name="step_attn",
    )(proj, *views, *tables)


def _ssd_kernel(z_ref, x_ref, bc_ref, dt_ref, dtt_ref, cw_ref, cb_ref, dtb_ref, dtbc_ref, nega_ref, negac_ref,
                dskip_ref, ng_ref, expand_ref, conv0_ref, s0_ref,
                y_ref, convn_ref, sfin_ref, ext_ref, st_ref, acc_ref, *, chunk):
    c = pl.program_id(1)
    tail = CONV_W - 1

    @pl.when(c == 0)
    def _():
        ext_ref[0:8] = jnp.concatenate([jnp.zeros((8 - tail, CONV_DIM_C), F32), conv0_ref[0]], axis=0)
        st_ref[...] = s0_ref[0]

    ext_ref[8:8 + chunk, :D_INNER_C] = x_ref[...]
    ext_ref[8:8 + chunk, D_INNER_C:] = bc_ref[...]
    cw = cw_ref[...]
    conv = cb_ref[...] + sum(cw[j:j + 1, :] * ext_ref[8 - tail + j:8 - tail + j + chunk, :] for j in range(CONV_W))
    xbc = _silu(conv)

    @pl.when(c == pl.num_programs(1) - 1)
    def _():
        convn_ref[0] = ext_ref[8 + chunk - tail:8 + chunk, :]

    ext_ref[0:8] = ext_ref[chunk:chunk + 8]
    xs = xbc[:, :D_INNER_C]
    gn = G_C * N_C
    bm = xbc[:, D_INNER_C:D_INNER_C + gn].astype(BF16)
    cm = xbc[:, D_INNER_C + gn:].astype(BF16)

    softplus = lambda a: jnp.maximum(a, 0.0) + jnp.log1p(jnp.exp(-jnp.abs(a)))
    dt = softplus(dt_ref[...] + dtb_ref[...])
    la = dt * nega_ref[...]
    la_t = softplus(dtt_ref[0, 0] + dtbc_ref[...]) * negac_ref[...]
    ii = _iota((chunk, chunk), 0)
    jj = _iota((chunk, chunk), 1)
    causal = ii >= jj
    if chunk % 16 == 0:
        select = lambda sel, a, lhs: _dot_select(sel.astype(BF16), a, lhs)
    else:
        select = lambda sel, a, lhs: _dot_exact(sel.astype(F32), a) if lhs else _dot_exact(a, sel.astype(F32))
    b = select(causal, la, True)
    b_t = select(ii <= jj, la_t, False)
    expand = expand_ref[...]
    b_x = select(expand, b, False)
    dt_x = select(expand, dt, False)
    b_last = b_x[chunk - 1:chunk, :]
    xdt = xs * dt_x
    xdt_b = xdt.astype(BF16)
    xdtw = (xdt * jnp.exp(b_last - b_x)).astype(BF16)
    e_b = jnp.exp(b_x)
    low = _iota((chunk, LANES), 1) < P_C
    hg = H_C // G_C
    wg = hg * P_C
    g_heads = jnp.broadcast_to(jnp.exp(b_t[:, chunk - 1:chunk]), (H_C, N_C))
    row_decay = jnp.concatenate([jnp.broadcast_to(g_heads[h:h + 1, :], (P_C, N_C)) for h in range(H_C)], axis=0)
    for g in range(G_C):
        cg, bg = cm[:, g * N_C:(g + 1) * N_C], bm[:, g * N_C:(g + 1) * N_C]
        gs = slice(g * wg, (g + 1) * wg)
        acc_ref[:, gs] = _nt(cg, st_ref[gs, :].astype(BF16)) * e_b[:, gs]
        scores = _nt(cg, bg)
        for hp in range(hg // 2):
            sl = slice(g * wg + hp * LANES, g * wg + (hp + 1) * LANES)
            res = []
            for h in (g * hg + 2 * hp, g * hg + 2 * hp + 1):
                decay = jnp.exp(jnp.where(causal, b[:, h:h + 1] - b_t[h:h + 1, :], MASK_NEG))
                res.append(_dot((scores * decay).astype(BF16), xdt_b[:, sl]))
            acc_ref[:, sl] += jnp.where(low, res[0], res[1])
        st_ref[gs, :] = st_ref[gs, :] * row_decay[gs, :] + _tn(xdtw[:, gs], bg)

    y = (acc_ref[...] + dskip_ref[...] * xs) * _silu(z_ref[...])
    for g in range(G_C):
        gs = slice(g * wg, (g + 1) * wg)
        yg = y[:, gs]
        y_ref[:, gs] = (yg * lax.rsqrt(jnp.mean(yg * yg, -1, keepdims=True) + RMS_EPS) * ng_ref[:, gs]).astype(BF16)

    @pl.when(c == pl.num_programs(1) - 1)
    def _():
        sfin_ref[0] = st_ref[...]


def ssd(proj, dt_t, prm, conv0_all, s0, o, bsz, t_len, chunk):
    nc = t_len // chunk
    blk = lambda w, j: pl.BlockSpec((chunk, w), lambda b, c, j=j: (b * nc + c, j))
    const = lambda a: pl.BlockSpec(a.shape, lambda b, c: (0,) * a.ndim)
    consts = [prm["conv_w"], prm["conv_b"], prm["dt_bias"], prm["dt_bias_col"], prm["neg_a"], prm["neg_a_col"],
              prm["d_skip"], prm["norm_g"], prm["expand"]]
    return pl.pallas_call(
        functools.partial(_ssd_kernel, chunk=chunk),
        grid=(bsz, nc),
        in_specs=[blk(D_INNER_C, 0), blk(D_INNER_C, 1), blk(2 * G_C * N_C, 4), blk(LANES, ODD_MAIN // LANES),
                  pl.BlockSpec((1, 1, H_C, chunk), lambda b, c: (b, c, 0, 0))]
        + [const(a) for a in consts]
        + [pl.BlockSpec((None, 1, CONV_W - 1, CONV_DIM_C), lambda b, c: (o, b, 0, 0)),
           pl.BlockSpec((None, 1, D_INNER_C, N_C), lambda b, c: (o, b, 0, 0))],
        out_specs=[pl.BlockSpec((chunk, D_INNER_C), lambda b, c: (b * nc + c, 0)),
                   pl.BlockSpec((1, CONV_W - 1, CONV_DIM_C), lambda b, c: (b, 0, 0)),
                   pl.BlockSpec((1, D_INNER_C, N_C), lambda b, c: (b, 0, 0))],
        out_shape=[jax.ShapeDtypeStruct((bsz * t_len, D_INNER_C), BF16),
                   jax.ShapeDtypeStruct((bsz, CONV_W - 1, CONV_DIM_C), F32),
                   jax.ShapeDtypeStruct((bsz, D_INNER_C, N_C), F32)],
        scratch_shapes=[pltpu.VMEM((chunk + 8, CONV_DIM_C), F32), pltpu.VMEM((D_INNER_C, N_C), F32),
                        pltpu.VMEM((chunk, D_INNER_C), F32)],
        compiler_params=_params("parallel", "arbitrary"),
        name="ssd",
    )(proj, proj, proj, proj, dt_t, *consts, conv0_all, s0)


def _retention_kernel(q_ref, k_ref, v_ref, g_ref, cf_ref, sa_ref, sb_ref, dmat_ref, gq_ref, gw_ref, gs_ref, s0_ref,
                      o_ref, sfin_ref, st_ref, *, chunk):
    c = pl.program_id(1)

    nseq = s0_ref.shape[0]
    wst = H_D * DK_D

    @pl.when(c == 0)
    def _():
        for i in range(nseq):
            st_ref[i * wst:(i + 1) * wst, :] = s0_ref[i].reshape(wst, DV_D)

    half = DK_D // 2
    cf, sa, sb = cf_ref[...], sa_ref[...], sb_ref[...]
    gate = _silu(g_ref[...])
    low = _iota((chunk, LANES), 1) < DK_D
    items = [(i, h) for i in range(nseq) for h in range(H_D)]
    seq = lambda i: slice(i * chunk, (i + 1) * chunk)
    pair = lambda a, h: a[:, (h // 2) * LANES:(h // 2 + 1) * LANES]
    sel = lambda h: low if h % 2 == 0 else ~low
    qr = [_rope(q_ref[seq(i), :], cf, sa, sb, half) for i in range(nseq)]
    kr = [_rope(k_ref[seq(i), :], cf, sa, sb, half) * DK_D ** -0.5 for i in range(nseq)]
    vh = {(i, h): v_ref[seq(i), h * DV_D:(h + 1) * DV_D].astype(BF16) for i, h in items}
    states = {(i, hp): st_ref[i * wst + hp * LANES:i * wst + (hp + 1) * LANES, :]
              for i in range(nseq) for hp in range(H_D // 2)}
    qm = {(i, h): jnp.where(sel(h), pair(qr[i], h), 0.0).astype(BF16) for i, h in items}
    km = {(i, h): (jnp.where(sel(h), pair(kr[i], h), 0.0) * gw_ref[h]).astype(BF16) for i, h in items}
    scores = {(i, h): _nt(qm[i, h], pair(kr[i], h).astype(BF16)) for i, h in items}
    inter = {(i, h): _dot(qm[i, h], states[i, h // 2].astype(BF16)) for i, h in items}
    upd = {(i, h): _tn(km[i, h], vh[i, h]) for i, h in items}
    atts = {(i, h): (scores[i, h] * dmat_ref[h]).astype(BF16) for i, h in items}
    intra = {(i, h): _dot(atts[i, h], vh[i, h]) for i, h in items}
    for h in range(H_D):
        cols = []
        for i in range(nseq):
            oh = intra[i, h] + inter[i, h] * gq_ref[h]
            mu = jnp.mean(oh, -1, keepdims=True)
            cen = oh - mu
            var = jnp.mean(cen * cen, -1, keepdims=True)
            cols.append(cen * lax.rsqrt(var + LN_EPS))
        o_ref[:, h * DV_D:(h + 1) * DV_D] = (jnp.concatenate(cols, axis=0) * gate[:, h * DV_D:(h + 1) * DV_D]).astype(BF16)
    for (i, hp), s in states.items():
        st_ref[i * wst + hp * LANES:i * wst + (hp + 1) * LANES, :] = s * gs_ref[hp] + upd[i, 2 * hp] + upd[i, 2 * hp + 1]

    @pl.when(c == pl.num_programs(1) - 1)
    def _():
        for i in range(nseq):
            sfin_ref[i] = st_ref[i * wst:(i + 1) * wst, :].reshape(H_D, DK_D, DV_D)


def retention(proj, tables, consts, s0_all, o, bsz, t_len, chunk, nseq=1):
    nc = t_len // chunk
    assert nseq == 1 or nc == 1
    blk = lambda w, j: pl.BlockSpec((nseq * chunk, w), lambda b, c, j=j: (b * nc + c, j))
    tab = pl.BlockSpec((chunk, LANES), lambda b, c: (c, 0))
    const = lambda a: pl.BlockSpec(a.shape, lambda b, c: (0,) * a.ndim)
    wqk = H_D * DK_D
    return pl.pallas_call(
        functools.partial(_retention_kernel, chunk=chunk),
        grid=(bsz // nseq, nc),
        in_specs=[blk(wqk, 2560 // wqk), blk(wqk, 2816 // wqk), blk(W_D, 3072 // W_D), blk(W_D, 3584 // W_D),
                  tab, tab, tab] + [const(a) for a in consts]
        + [pl.BlockSpec((None, nseq, H_D, DK_D, DV_D), lambda b, c: (o, b, 0, 0, 0))],
        out_specs=[pl.BlockSpec((nseq * chunk, W_D), lambda b, c: (b * nc + c, 0)),
                   pl.BlockSpec((nseq, H_D, DK_D, DV_D), lambda b, c: (b, 0, 0, 0))],
        out_shape=[jax.ShapeDtypeStruct((bsz * t_len, W_D), BF16),
                   jax.ShapeDtypeStruct((bsz, H_D, DK_D, DV_D), F32)],
        scratch_shapes=[pltpu.VMEM((nseq * H_D * DK_D, DV_D), F32)],
        compiler_params=_params("parallel", "arbitrary"),
        name="retention",
    )(proj, proj, proj, proj, *tables, *consts, s0_all)


def _retention_consts(chunk):
    lg = np.log1p(-np.exp2(-5.0 - np.arange(H_D, dtype=np.float64)))
    t = np.arange(chunk)
    diff = t[:, None] - t[None, :]
    dmat = np.where(diff >= 0, np.exp(lg[:, None, None] * diff), 0.0)
    gq = np.broadcast_to(np.exp(lg[:, None, None] * (t[None, :, None] + 1)), (H_D, chunk, DV_D))
    gw = np.broadcast_to(np.exp(lg[:, None, None] * (chunk - 1 - t[None, :, None])), (H_D, chunk, LANES))
    gs = np.repeat(np.exp(lg * chunk), DK_D).reshape(H_D // 2, 2 * DK_D, 1)
    gs = np.broadcast_to(gs, (H_D // 2, 2 * DK_D, DV_D))
    return [jnp.asarray(a, F32) for a in (dmat, gq, gw, gs)]


MEM_LANE_TILES = DH_M // LANES
MEM_ROW_PERIOD = 2 * MEM_LANE_TILES * H_M


def mem_kv_rows(kv):
    lead = kv.shape[:-4]
    kv = kv.reshape(*lead, N_MEM, 2, H_M, MEM_LANE_TILES, LANES)
    return jnp.swapaxes(kv, -3, -2).reshape(*lead, N_MEM * MEM_ROW_PERIOD, LANES)


def _mem_attn_kernel(q_ref, kv_ref, o_ref):
    nseq = kv_ref.shape[0]
    rows = q_ref.shape[0] // nseq

    def head_rows(i, which, h):
        tiles = [kv_ref[i, pl.ds((which * MEM_LANE_TILES + j) * H_M + h, N_MEM, stride=MEM_ROW_PERIOD), :]
                 for j in range(MEM_LANE_TILES)]
        return jnp.concatenate(tiles, axis=1).astype(BF16)

    items = [(i, h) for i in range(nseq) for h in range(H_M)]
    block = lambda i, h: (slice(i * rows, (i + 1) * rows), slice(h * DH_M, (h + 1) * DH_M))
    scores = [_nt(q_ref[block(i, h)].astype(BF16), head_rows(i, 0, h)) * DH_M ** -0.5 for i, h in items]
    probs, dens = [], []
    for s in scores:
        p = jnp.exp(s - jnp.max(s, -1, keepdims=True))
        dens.append(jnp.sum(p, -1, keepdims=True))
        probs.append(p.astype(BF16))
    outs = [_dot(p, head_rows(i, 1, h)) for (i, h), p in zip(items, probs)]
    normed = {item: o / den for item, o, den in zip(items, outs, dens)}
    for h in range(H_M):
        col = jnp.concatenate([normed[(i, h)] for i in range(nseq)], axis=0)
        o_ref[:, h * DH_M:(h + 1) * DH_M] = col.astype(BF16)


MEM_SEQS_PER_STEP = 4


def mem_attn(q, kv, layer, bsz, t_len):
    tq = min(t_len, 512)
    nq = t_len // tq
    nseq = MEM_SEQS_PER_STEP if (nq == 1 and tq * MEM_SEQS_PER_STEP <= 512 and bsz % MEM_SEQS_PER_STEP == 0) else 1
    tq *= nseq
    return pl.pallas_call(
        _mem_attn_kernel,
        grid=(bsz // nseq, nq),
        in_specs=[pl.BlockSpec((tq, D_MODEL), lambda b, i: (b * nq + i, 0)),
                  pl.BlockSpec((None, nseq, N_MEM * MEM_ROW_PERIOD, LANES), lambda b, i: (layer, b, 0, 0))],
        out_specs=pl.BlockSpec((tq, D_MODEL), lambda b, i: (b * nq + i, 0)),
        out_shape=jax.ShapeDtypeStruct((bsz * t_len, D_MODEL), BF16),
        compiler_params=_params("parallel", "arbitrary"),
        name="mem_attn",
    )(q, kv)


def _even_layer(x, xb, e, w_in, w_out, lb, gn, ln_g, ln_b, state_a, bsz, t_len, tables, caches):
    proj = matmul(xb, w_in, 512)
    if caches is None:
        o_a, s_a = hgrn2(proj, lb, gn, state_a, 0, bsz, t_len, GLA_CHUNK, GLA_SUB, GLA_ROWS)
        res = [band_attn(proj, tables, g, bsz, t_len) for g in range(N_BG)]
        o_b = merge_groups([r[0] for r in res], [r[1] for r in res], proj)
        rows = [window_rows(proj, tables, g, bsz, t_len) for g in range(N_BG)]
    else:
        o_a, s_a = hgrn2(proj, lb, gn, state_a, e, bsz, t_len, t_len, t_len, t_len, STEP_SEQS if bsz % STEP_SEQS == 0 else 1)
        o_b, *rows = step_attn(proj, caches, tables, e, bsz, t_len)
    x, xb = outproj_ln([o_a, o_b], [w_out[:W_A], w_out[W_A:]], x, ln_g, ln_b)
    rows = [r.reshape(bsz, r.shape[1], 2, H_B, DH_B) for r in rows]
    return x, xb, s_a, rows


def _odd_layer(x, xb, o, w_in, w_out, prm, ln_g, ln_b, s_c0, conv0, s_d0, bsz, t_len, chunk, tables, ret_consts):
    proj = matmul(xb, w_in, 384)
    nc = t_len // chunk
    dt_t = proj[:, ODD_MAIN:ODD_MAIN + H_C].reshape(bsz, nc, chunk, H_C).transpose(0, 1, 3, 2)
    y_c, conv_n, s_c = ssd(proj, dt_t, prm, conv0, s_c0, o, bsz, t_len, chunk)
    nseq = STEP_SEQS if (nc == 1 and bsz % STEP_SEQS == 0) else 1
    o_d, s_d = retention(proj, tables, ret_consts, s_d0, o, bsz, t_len, chunk, nseq)
    x, xb = outproj_ln([y_c, o_d], [w_out[:D_INNER_C], w_out[D_INNER_C:]], x, ln_g, ln_b)
    return x, xb, s_c, conv_n, s_d


def _memory_layer(x, xb, layer, wq, wo, kv, ln_g, ln_b, bsz, t_len):
    q = matmul(xb, wq, 512)
    att = mem_attn(q, kv, layer, bsz, t_len)
    return outproj_ln([att], [wo], x, ln_g, ln_b)


def _state_to_rows(s):
    return jnp.swapaxes(s, -1, -2).reshape(*s.shape[:-3], D_INNER_C, N_C)


def _state_from_rows(s):
    return jnp.swapaxes(s.reshape(*s.shape[:-2], H_C, P_C, N_C), -1, -2)


def kernel(x_prompt, x_sample, state_a, cache_b1, cache_b2, cache_b3, state_c_ssm, state_c_conv, state_d,
           cache_mem_kv, mem_prompt, w_in_even, a_lb_logits, a_norm_g, w_out_even, w_in_odd, c_conv_w, c_conv_b,
           c_dt_bias, c_a_log, c_d_skip, c_norm_g, w_out_odd, ln1_g, ln1_b, ln2_g, ln2_b, m_wq, m_wkv, m_wo):
    bp, tp = x_prompt.shape[0], x_prompt.shape[1]
    bs, ts = x_sample.shape[0], x_sample.shape[1]
    pos_p = jnp.arange(tp, dtype=jnp.int32)
    pos_s = PAST_LEN + jnp.arange(ts, dtype=jnp.int32)
    tab_b = {"p": _rope_tables(pos_p, ROT_DIM_B, DH_B, ROPE_THETA), "s": _rope_tables(pos_s, ROT_DIM_B, DH_B, ROPE_THETA)}
    tab_d = {"p": _rope_tables(pos_p, DK_D, DK_D, RET_THETA), "s": _rope_tables(pos_s, DK_D, DK_D, RET_THETA)}
    ret_consts = {"p": _retention_consts(RET_CHUNK), "s": _retention_consts(ts)}

    sm = jax.nn.softmax(a_lb_logits.astype(F32), axis=0)
    lb_all = jnp.cumsum(sm, axis=0) - sm[0]
    expand = jnp.asarray(np.pad(np.repeat(np.eye(H_C), P_C, axis=1), ((0, LANES - H_C), (0, 0))), F32)
    pad_h = lambda a: jnp.pad(a.astype(F32), (0, LANES - H_C))[None, :]
    dt0 = D_INNER_C + CONV_DIM_C
    zeros_a = jnp.zeros((1, bp, H_A, DK_A, DV_A), F32)
    zeros_c = jnp.zeros((1, bp, D_INNER_C, N_C), F32)
    ssm_rows = _state_to_rows(state_c_ssm)
    zeros_conv = jnp.zeros((1, bp, CONV_W - 1, CONV_DIM_C), F32)
    zeros_d = jnp.zeros((1, bp, H_D, DK_D, DV_D), F32)
    caches = (cache_b1, cache_b2, cache_b3)
    mem_kv_s = mem_kv_rows(cache_mem_kv)

    xp, xs = x_prompt.reshape(bp * tp, D_MODEL), x_sample.reshape(bs * ts, D_MODEL)
    xpb, xsb = xp.astype(BF16), xs.astype(BF16)
    mem_b = mem_prompt.reshape(bp * N_MEM, D_MODEL).astype(BF16)
    a_p, a_s, c_p, c_s, cv_p, cv_s, d_p, d_s, mem_p = [], [], [], [], [], [], [], [], []
    b_p = [[] for _ in range(N_BG)]
    b_s = [[] for _ in range(N_BG)]
    for l in range(DEPTH):
        if l % 2 == 0:
            e = l // 2
            w_in, w_out = w_in_even[e].astype(BF16), w_out_even[e].astype(BF16)
            lb = lb_all[e][None, :]
            gn = jnp.tile(a_norm_g[e].astype(F32), H_A)[None, :]
            g1, b1 = ln1_g[l][None, :], ln1_b[l][None, :]
            xp, xpb, sap, rp = _even_layer(xp, xpb, e, w_in, w_out, lb, gn, g1, b1, zeros_a, bp, tp, tab_b["p"], None)
            xs, xsb, sas, rs = _even_layer(xs, xsb, e, w_in, w_out, lb, gn, g1, b1, state_a, bs, ts, tab_b["s"], caches)
            a_p.append(sap)
            a_s.append(sas)
            for g in range(N_BG):
                b_p[g].append(rp[g])
                b_s[g].append(rs[g])
        else:
            o = l // 2
            w = w_in_odd[o]
            w_in = jnp.concatenate([w[:, :dt0], w[:, dt0 + H_C:], w[:, dt0:dt0 + H_C],
                                    jnp.zeros((D_MODEL, LANES - H_C), w.dtype)], axis=1).astype(BF16)
            w_out = w_out_odd[o].astype(BF16)
            prm = {
                "conv_w": c_conv_w[o].astype(F32), "conv_b": c_conv_b[o].astype(F32)[None, :],
                "dt_bias": pad_h(c_dt_bias[o]), "dt_bias_col": c_dt_bias[o].astype(F32)[:, None],
                "neg_a": pad_h(-jnp.exp(c_a_log[o].astype(F32))), "neg_a_col": -jnp.exp(c_a_log[o].astype(F32))[:, None],
                "d_skip": jnp.repeat(c_d_skip[o].astype(F32), P_C)[None, :], "norm_g": c_norm_g[o].astype(F32)[None, :],
                "expand": expand,
            }
            g1, b1 = ln1_g[l][None, :], ln1_b[l][None, :]
            xp, xpb, scp, cvp, sdp = _odd_layer(xp, xpb, 0, w_in, w_out, prm, g1, b1, zeros_c, zeros_conv, zeros_d,
                                                bp, tp, SSD_CHUNK, tab_d["p"], ret_consts["p"])
            xs, xsb, scs, cvs, sds = _odd_layer(xs, xsb, o, w_in, w_out, prm, g1, b1, ssm_rows,
                                                state_c_conv, state_d, bs, ts, ts, tab_d["s"], ret_consts["s"])
            c_p.append(_state_from_rows(scp))
            c_s.append(_state_from_rows(scs))
            cv_p.append(cvp)
            cv_s.append(cvs)
            d_p.append(sdp)
            d_s.append(sds)
        kv_p = matmul(mem_b, m_wkv[l].astype(BF16), 512)
        kv_p = kv_p.reshape(bp, N_MEM, 2, H_M, DH_M)
        mem_p.append(kv_p)
        wq, wo = m_wq[l].astype(BF16), m_wo[l].astype(BF16)
        g2, b2 = ln2_g[l][None, :], ln2_b[l][None, :]
        xp, xpb = _memory_layer(xp, xpb, 0, wq, wo, mem_kv_rows(kv_p)[None], g2, b2, bp, tp)
        xs, xsb = _memory_layer(xs, xsb, l, wq, wo, mem_kv_s, g2, b2, bs, ts)
    return (xp.reshape(bp, tp, D_MODEL), xs.reshape(bs, ts, D_MODEL), jnp.stack(a_p), jnp.stack(a_s),
            jnp.stack(b_p[0]), jnp.stack(b_p[1]), jnp.stack(b_p[2]),
            jnp.stack(b_s[0]), jnp.stack(b_s[1]), jnp.stack(b_s[2]),
            jnp.stack(c_p), jnp.stack(c_s), jnp.stack(cv_p), jnp.stack(cv_s),
            jnp.stack(d_p), jnp.stack(d_s), jnp.stack(mem_p))
```
